```python
import jax, jax.numpy as jnp
from jax import lax
import numpy as np

D_MODEL = 2048
BATCH = 16
SEQ = 2048
DEPTH = 4

CHUNK = 64
N_META = 16
EPS = 1e-6
D_FF = 5632
D_CONV = D_MODEL // 2
CONV_W = 3
D_RWKV = D_MODEL - D_CONV
HEAD_RWKV = 64
H_RWKV = D_RWKV // HEAD_RWKV
DECAY_RANK = max(32, int(round(1.8 * D_RWKV ** 0.5 / 32)) * 32)
AAA_RANK = max(32, int(round(1.8 * D_RWKV ** 0.5 / 32)) * 32)
GATE_RANK = max(32, int(round(0.6 * D_RWKV ** 0.8 / 32)) * 32)
RWKV_PROJ = 3 * D_RWKV + DECAY_RANK + AAA_RANK + GATE_RANK
EVEN_PROJ = 3 * D_CONV + RWKV_PROJ
LNX_EPS = 64e-5
HEAD_FOX = 128
H_FOX = D_MODEL // HEAD_FOX
ODD_PROJ = 4 * D_MODEL + H_FOX
Q_BLOCK = 128
FORGET_BIAS = 2.0
N_EVEN = (DEPTH + 1) // 2
N_ODD = DEPTH // 2

kernel_name = 'hybrid_conv_rwkv7_fox_macaron_encoder'


def rmsnorm(x, g):
    xf = x.astype(jnp.float32)
    y = xf * lax.rsqrt(jnp.mean(xf * xf, axis=-1, keepdims=True) + EPS)
    return (y * g.astype(jnp.float32)).astype(x.dtype)


def swiglu(x, w_in, w_out):
    gu = x @ w_in
    return (jax.nn.silu(gu[..., :D_FF]) * gu[..., D_FF:]) @ w_out


def rwkv7_scan(r, w, k, v, a, b):
    bsz, _, h, n = r.shape
    xs = tuple(jnp.moveaxis(t, 1, 0) for t in (r, w, k, v, a, b))

    def step(S, inp):
        r_t, w_t, k_t, v_t, a_t, b_t = inp
        sa = jnp.einsum('bhvk,bhk->bhv', S, a_t)
        S = S * w_t[:, :, None, :] + sa[..., None] * b_t[:, :, None, :] + v_t[..., None] * k_t[:, :, None, :]
        return S, jnp.einsum('bhvk,bhk->bhv', S, r_t)

    S0 = jnp.zeros((bsz, h, n, n), jnp.float32)
    _, y = lax.scan(step, S0, xs)
    return jnp.moveaxis(y, 0, 1)


def conv_rwkv_mixer(xn, w_in, conv_w, mu, w0, w2, a0, a2, g2, k_k, k_a, r_k, lnx_g, lnx_b, w_out):
    bsz, L, _ = xn.shape
    p = xn @ w_in
    gate_b = p[..., :D_CONV]
    gate_c = p[..., D_CONV:2 * D_CONV]
    hv = p[..., 2 * D_CONV:3 * D_CONV]
    z = gate_c * hv
    zp = jnp.pad(z, ((0, 0), (CONV_W - 1, 0), (0, 0)))
    conv = sum(conv_w[j] * zp[:, j:j + L] for j in range(CONV_W))
    y_a = gate_b * conv
    pb = p[..., 3 * D_CONV:]
    pb_prev = jnp.pad(pb, ((0, 0), (1, 0), (0, 0)))[:, :L]
    u = pb + (pb_prev - pb) * mu
    o = 0
    r = u[..., o:o + D_RWKV]; o += D_RWKV
    k = u[..., o:o + D_RWKV]; o += D_RWKV
    v = u[..., o:o + D_RWKV]; o += D_RWKV
    uw = u[..., o:o + DECAY_RANK]; o += DECAY_RANK
    ua = u[..., o:o + AAA_RANK]; o += AAA_RANK
    ug = u[..., o:o + GATE_RANK]
    w_log = -jax.nn.softplus(-(w0 + jnp.tanh(uw) @ w2).astype(jnp.float32)) - 0.5
    decay = jnp.exp(-jnp.exp(w_log))
    a = jax.nn.sigmoid((a0 + ua @ a2).astype(jnp.float32))
    g = jax.nn.sigmoid(ug) @ g2
    heads = lambda t: t.reshape(bsz, L, H_RWKV, HEAD_RWKV)
    kk = heads((k * k_k).astype(jnp.float32))
    kk = kk / jnp.maximum(jnp.linalg.norm(kk, axis=-1, keepdims=True), 1e-12)
    kf = k.astype(jnp.float32) * (1.0 + (a - 1.0) * k_a.astype(jnp.float32))
    rf = r.astype(jnp.float32)
    vf = v.astype(jnp.float32)
    y = rwkv7_scan(heads(rf), heads(decay), heads(kf), heads(vf), -kk, kk * heads(a))
    mean = jnp.mean(y, axis=-1, keepdims=True)
    var = jnp.mean(jnp.square(y - mean), axis=-1, keepdims=True)
    yn = ((y - mean) * lax.rsqrt(var + LNX_EPS)).reshape(bsz, L, D_RWKV)
    yn = yn * lnx_g.astype(jnp.float32) + lnx_b.astype(jnp.float32)
    bonus = jnp.sum(heads(rf * kf * r_k.astype(jnp.float32)), axis=-1, keepdims=True) * heads(vf)
    y_b = ((yn + bonus.reshape(bsz, L, D_RWKV)) * g.astype(jnp.float32)).astype(xn.dtype)
    return jnp.concatenate([y_a, y_b], axis=-1) @ w_out


def forgetting_attention(q, k, v, c):
    L = q.shape[1]
    dh = q.shape[-1]
    nb = -(-L // Q_BLOCK)
    lp = nb * Q_BLOCK
    pad4 = ((0, 0), (0, lp - L), (0, 0), (0, 0))
    q, k, v = jnp.pad(q, pad4), jnp.pad(k, pad4), jnp.pad(v, pad4)
    cT = jnp.pad(c, ((0, 0), (0, lp - L), (0, 0))).transpose(0, 2, 1)
    scale = dh ** -0.5
    outs = []
    for i in range(nb):
        q0, q1 = i * Q_BLOCK, (i + 1) * Q_BLOCK
        s = jnp.einsum('bqhd,bkhd->bhqk', q[:, q0:q1], k[:, :q1]).astype(jnp.float32) * scale
        s = s + cT[:, :, q0:q1, None] - cT[:, :, None, :q1]
        mask = np.arange(q0, q1)[:, None] >= np.arange(q1)[None, :]
        s = jnp.where(mask, s, -jnp.inf)
        pr = jax.nn.softmax(s, axis=-1).astype(v.dtype)
        outs.append(jnp.einsum('bhqk,bkhd->bqhd', pr, v[:, :q1]))
    return jnp.concatenate(outs, axis=1)[:, :L]


def fox_mixer(xn, w_in, b_f, q_g, k_g, w_out):
    bsz, L, _ = xn.shape
    p = xn @ w_in
    heads = lambda t: t.reshape(bsz, L, H_FOX, HEAD_FOX)
    q = rmsnorm(heads(p[..., :D_MODEL]), q_g)
    k = rmsnorm(heads(p[..., D_MODEL:2 * D_MODEL]), k_g)
    v = heads(p[..., 2 * D_MODEL:3 * D_MODEL])
    og = p[..., 3 * D_MODEL:4 * D_MODEL]
    fl = p[..., 4 * D_MODEL:]
    logf = jax.nn.log_sigmoid(fl.astype(jnp.float32) + b_f.astype(jnp.float32))
    c = jnp.cumsum(logf, axis=1)
    o = forgetting_attention(q, k, v, c).reshape(bsz, L, D_MODEL)
    return (o * jax.nn.sigmoid(og)) @ w_out


def setup_inputs(seed: int = 0) -> dict:
    key = jax.random.key(seed)
    ks = jax.random.split(key, 32)
    nrm = lambda kk, shape, sc: jax.random.normal(kk, shape, jnp.float32) * sc
    D = D_MODEL
    return {
        'x': nrm(ks[0], (BATCH, SEQ, D), 1.0),
        'meta': nrm(ks[1], (N_META, D), 1.0),
        'norm_g': 1.0 + nrm(ks[2], (DEPTH, 6, D), 0.05),
        'ffn_in': nrm(ks[3], (DEPTH, 2, D, 2 * D_FF), D ** -0.5),
        'ffn_out': nrm(ks[4], (DEPTH, 2, D_FF, D), D_FF ** -0.5),
        'e_w_in': nrm(ks[5], (N_EVEN, D, EVEN_PROJ), D ** -0.5),
        'e_conv_w': nrm(ks[6], (N_EVEN, CONV_W, D_CONV), CONV_W ** -0.5),
        'e_mu': jax.random.uniform(ks[7], (N_EVEN, RWKV_PROJ), jnp.float32),
        'e_w0': nrm(ks[8], (N_EVEN, D_RWKV), 0.5),
        'e_w2': nrm(ks[9], (N_EVEN, DECAY_RANK, D_RWKV), DECAY_RANK ** -0.5),
        'e_a0': nrm(ks[10], (N_EVEN, D_RWKV), 0.1),
        'e_a2': nrm(ks[11], (N_EVEN, AAA_RANK, D_RWKV), AAA_RANK ** -0.5),
        'e_g2': nrm(ks[12], (N_EVEN, GATE_RANK, D_RWKV), GATE_RANK ** -0.5),
        'e_k_k': 0.85 + nrm(ks[13], (N_EVEN, D_RWKV), 0.05),
        'e_k_a': 1.0 + nrm(ks[14], (N_EVEN, D_RWKV), 0.05),
        'e_r_k': nrm(ks[15], (N_EVEN, D_RWKV), 0.1),
        'e_lnx_g': 1.0 + nrm(ks[16], (N_EVEN, D_RWKV), 0.05),
        'e_lnx_b': nrm(ks[17], (N_EVEN, D_RWKV), 0.02),
        'e_w_out': nrm(ks[18], (N_EVEN, D_CONV + D_RWKV, D), (D_CONV + D_RWKV) ** -0.5),
        'o_w_in': nrm(ks[19], (N_ODD, D, ODD_PROJ), D ** -0.5),
        'o_b_f': FORGET_BIAS + nrm(ks[20], (N_ODD, H_FOX), 0.5),
        'o_q_g': 1.0 + nrm(ks[21], (N_ODD, HEAD_FOX), 0.05),
        'o_k_g': 1.0 + nrm(ks[22], (N_ODD, HEAD_FOX), 0.05),
        'o_w_out': nrm(ks[23], (N_ODD, D, D), D ** -0.5),
    }


def reference(x, meta, norm_g, ffn_in, ffn_out, e_w_in, e_conv_w, e_mu, e_w0, e_w2, e_a0, e_a2,
              e_g2, e_k_k, e_k_a, e_r_k, e_lnx_g, e_lnx_b, e_w_out, o_w_in, o_b_f, o_q_g, o_k_g,
              o_w_out):
    bsz = x.shape[0]
    h = jnp.concatenate([jnp.broadcast_to(meta[None].astype(x.dtype), (bsz, N_META, D_MODEL)), x], axis=1)
    for l in range(DEPTH):
        g = norm_g[l]
        h = h + 0.5 * rmsnorm(swiglu(rmsnorm(h, g[0]), ffn_in[l, 0], ffn_out[l, 0]), g[1])
        xn = rmsnorm(h, g[2])
        i = l // 2
        if l % 2 == 0:
            m = conv_rwkv_mixer(xn, e_w_in[i], e_conv_w[i], e_mu[i], e_w0[i], e_w2[i], e_a0[i], e_a2[i],
                                e_g2[i], e_k_k[i], e_k_a[i], e_r_k[i], e_lnx_g[i], e_lnx_b[i], e_w_out[i])
        else:
            m = fox_mixer(xn, o_w_in[i], o_b_f[i], o_q_g[i], o_k_g[i], o_w_out[i])
        h = h + rmsnorm(m, g[3])
        h = h + 0.5 * rmsnorm(swiglu(rmsnorm(h, g[4]), ffn_in[l, 1], ffn_out[l, 1]), g[5])
    return h[:, N_META:]
```

```python
import functools
import math

import jax
import jax.numpy as jnp
from jax import lax
from jax.experimental import pallas as pl
from jax.experimental.pallas import tpu as pltpu

F32 = jnp.float32
BF16 = jnp.bfloat16

N_META = 16
EPS = 1e-6
LNX_EPS = 64e-5
HEAD_RWKV = 64
HEAD_FOX = 128
CONV_W = 3
Q_BLOCK = 128

LANES = 128
VMEM_LIMIT = 56 * 1024 * 1024


def _params(*sem):
    return pltpu.CompilerParams(dimension_semantics=sem, vmem_limit_bytes=VMEM_LIMIT)


def _pick_tile(n, candidates):
    for c in candidates:
        if n % c == 0:
            return c
    return n


def _rms_scale(x):
    return lax.rsqrt(jnp.mean(x * x, axis=-1, keepdims=True) + EPS)


def _sigmoid(x):
    return 1.0 / (1.0 + jnp.exp(-x))


def _ffn_kernel(x_ref, gi_ref, wg_ref, wu_ref, wo_ref, go_ref, o_ref, xn_ref):
    k = pl.program_id(1)
    nk = pl.num_programs(1)

    @pl.when(k == 0)
    def _():
        x = x_ref[...]
        xn_ref[...] = (x * _rms_scale(x) * gi_ref[...]).astype(BF16)

    xn = xn_ref[...]
    gate = jnp.dot(xn, wg_ref[...], preferred_element_type=F32)
    up = jnp.dot(xn, wu_ref[...], preferred_element_type=F32)
    mid = (gate * _sigmoid(gate) * up).astype(BF16)
    part = jnp.dot(mid, wo_ref[...], preferred_element_type=F32)

    @pl.when(k == 0)
    def _():
        o_ref[...] = part

    @pl.when(k > 0)
    def _():
        o_ref[...] += part

    @pl.when(k == nk - 1)
    def _():
        y = o_ref[...]
        o_ref[...] = x_ref[...] + 0.5 * (y * _rms_scale(y) * go_ref[...])


def _ffn(h, g_in, w_in, w_out, g_out):
    t, d = h.shape
    d_ff = w_out.shape[0]
    tm = _pick_tile(t, (768, 512, 384, 256, 128, 64, 32, 16))
    tf = _pick_tile(d_ff, (512, 256, 128))
    nk = d_ff // tf
    return pl.pallas_call(
        _ffn_kernel,
        grid=(t // tm, nk),
        in_specs=[
            pl.BlockSpec((tm, d), lambda i, k: (i, 0)),
            pl.BlockSpec((1, d), lambda i, k: (0, 0)),
            pl.BlockSpec((d, tf), lambda i, k: (0, k)),
            pl.BlockSpec((d, tf), lambda i, k: (0, nk + k)),
            pl.BlockSpec((tf, d), lambda i, k: (k, 0)),
            pl.BlockSpec((1, d), lambda i, k: (0, 0)),
        ],
        out_specs=pl.BlockSpec((tm, d), lambda i, k: (i, 0)),
        out_shape=jax.ShapeDtypeStruct((t, d), F32),
        scratch_shapes=[pltpu.VMEM((tm, d), BF16)],
        compiler_params=_params("parallel", "arbitrary"),
        name="ffn",
    )(h, g_in.reshape(1, d), w_in, w_in, w_out, g_out.reshape(1, d))


def _proj_kernel(x_ref, g_ref, wm_ref, wt_ref, pm_ref, pt_ref, xn_ref):
    j = pl.program_id(1)

    @pl.when(j == 0)
    def _():
        x = x_ref[...]
        xn = (x * _rms_scale(x) * g_ref[...]).astype(BF16)
        xn_ref[...] = xn
        pt_ref[...] = jnp.dot(xn, wt_ref[...], preferred_element_type=F32)

    pm_ref[...] = jnp.dot(xn_ref[...], wm_ref[...], preferred_element_type=F32)


def _proj(h, g, w_main, w_tail):
    t, d = h.shape
    n_main = w_main.shape[1]
    n_tail = w_tail.shape[1]
    tm = _pick_tile(t, (768, 512, 384, 256, 128, 64, 32, 16))
    tn = _pick_tile(n_main, (1024, 512, 256, 128))
    return pl.pallas_call(
        _proj_kernel,
        grid=(t // tm, n_main // tn),
        in_specs=[
            pl.BlockSpec((tm, d), lambda i, j: (i, 0)),
            pl.BlockSpec((1, d), lambda i, j: (0, 0)),
            pl.BlockSpec((d, tn), lambda i, j: (0, j)),
            pl.BlockSpec((d, n_tail), lambda i, j: (0, 0)),
        ],
        out_specs=[
            pl.BlockSpec((tm, tn), lambda i, j: (i, j)),
            pl.BlockSpec((tm, n_tail), lambda i, j: (i, 0)),
        ],
        out_shape=[
            jax.ShapeDtypeStruct((t, n_main), F32),
            jax.ShapeDtypeStruct((t, n_tail), F32),
        ],
        scratch_shapes=[pltpu.VMEM((tm, d), BF16)],
        compiler_params=_params("parallel", "arbitrary"),
        name="proj",
    )(h, g.reshape(1, d), w_main, w_tail)


def _out_even_kernel(h_ref, ya_ref, yb_ref, gt_ref, wa_ref, wb_ref, g_ref, o_ref):
    yb = (yb_ref[...] * gt_ref[...]).astype(BF16)
    m = jnp.dot(ya_ref[...], wa_ref[...], preferred_element_type=F32)
    m = m + jnp.dot(yb, wb_ref[...], preferred_element_type=F32)
    o_ref[...] = h_ref[...] + m * _rms_scale(m) * g_ref[...]


def _out_even(h, y_a, y_b, gate, w_out, g):
    t, d = h.shape
    da = y_a.shape[1]
    db = y_b.shape[1]
    tm = _pick_tile(t, (384, 256, 128, 64, 32, 16))
    return pl.pallas_call(
        _out_even_kernel,
        grid=(t // tm,),
        in_specs=[
            pl.BlockSpec((tm, d), lambda i: (i, 0)),
            pl.BlockSpec((tm, da), lambda i: (i, 0)),
            pl.BlockSpec((tm, db), lambda i: (i, 0)),
            pl.BlockSpec((tm, db), lambda i: (i, 0)),
            pl.BlockSpec((da, d), lambda i: (0, 0)),
            pl.BlockSpec((db, d), lambda i: (0, 0)),
            pl.BlockSpec((1, d), lambda i: (0, 0)),
        ],
        out_specs=pl.BlockSpec((tm, d), lambda i: (i, 0)),
        out_shape=jax.ShapeDtypeStruct((t, d), F32),
        compiler_params=_params("parallel"),
        name="out_even",
    )(h, y_a, y_b, gate, w_out[:da], w_out[da:], g.reshape(1, d))


def _out_odd_kernel(h_ref, o_in_ref, og_ref, w_ref, g_ref, o_ref):
    act = (o_in_ref[...].astype(F32) * _sigmoid(og_ref[...])).astype(BF16)
    m = jnp.dot(act, w_ref[...], preferred_element_type=F32)
    o_ref[...] = h_ref[...] + m * _rms_scale(m) * g_ref[...]


def _out_odd(h, o, p_main, og_block, w_out, g):
    t, d = h.shape
    tm = _pick_tile(t, (384, 256, 128, 64, 32, 16))
    return pl.pallas_call(
        _out_odd_kernel,
        grid=(t // tm,),
        in_specs=[
            pl.BlockSpec((tm, d), lambda i: (i, 0)),
            pl.BlockSpec((tm, d), lambda i: (i, 0)),
            pl.BlockSpec((tm, d), lambda i: (i, og_block)),
            pl.BlockSpec((d, d), lambda i: (0, 0)),
            pl.BlockSpec((1, d), lambda i: (0, 0)),
        ],
        out_specs=pl.BlockSpec((tm, d), lambda i: (i, 0)),
        out_shape=jax.ShapeDtypeStruct((t, d), F32),
        compiler_params=_params("parallel"),
        name="out_odd",
    )(h, o, p_main, w_out, g.reshape(1, d))


def _shift_rows(x, n):
    rows = lax.broadcasted_iota(jnp.int32, x.shape, 0)
    return jnp.where(rows < n, 0.0, pltpu.roll(x, n, 0))


def _even_prep_kernel(gb_ref, gc_ref, hv_ref, pr_ref, pk_ref, pv_ref, pl_ref,
                      cw_ref, mur_ref, muk_ref, muv_ref, mul_ref,
                      w0_ref, w2_ref, a0_ref, a2_ref, g2_ref, kk_ref, ka_ref,
                      ya_ref, r_ref, w_ref, kf_ref, v_ref, kr_ref, a_ref, g_ref):
    z = gc_ref[...] * hv_ref[...]
    cw = cw_ref[...]
    conv = cw[0:1] * _shift_rows(z, 2) + cw[1:2] * _shift_rows(z, 1) + cw[2:3] * z
    ya_ref[...] = (gb_ref[...] * conv).astype(BF16)

    def lerp(ref, mu_ref):
        cur = ref[...]
        return cur + (_shift_rows(cur, 1) - cur) * mu_ref[...]

    r = lerp(pr_ref, mur_ref)
    k = lerp(pk_ref, muk_ref)
    v = lerp(pv_ref, muv_ref)
    ul = lerp(pl_ref, mul_ref)
    xw = w0_ref[...] + jnp.dot(jnp.tanh(ul).astype(BF16), w2_ref[...], preferred_element_type=F32)
    w_ref[...] = jnp.exp(-math.exp(-0.5) * _sigmoid(xw))
    a = _sigmoid(a0_ref[...] + jnp.dot(ul.astype(BF16), a2_ref[...], preferred_element_type=F32))
    g_ref[...] = jnp.dot(_sigmoid(ul).astype(BF16), g2_ref[...], preferred_element_type=F32)
    r_ref[...] = r
    v_ref[...] = v
    a_ref[...] = a
    kr_ref[...] = k * kk_ref[...]
    kf_ref[...] = k * (1.0 + (a - 1.0) * ka_ref[...])


def _even_prep(p_main, p_tail, conv_w, mu, w0, w2, a0, a2, g2, k_k, k_a, bsz, seq):
    d_conv = conv_w.shape[1]
    d_rwkv = w0.shape[0]
    decay_rank, aaa_rank, gate_rank = w2.shape[0], a2.shape[0], g2.shape[0]
    n_tail = p_tail.shape[-1]
    tc = LANES
    nc = d_rwkv // tc
    assert d_conv == d_rwkv and d_conv % tc == 0
    pm = p_main.reshape(bsz, seq, -1)
    pt = p_tail.reshape(bsz, seq, n_tail)
    cb = d_conv // tc

    def seg(s):
        return pl.BlockSpec((None, seq, tc), lambda b, c, s=s: (b, 0, s * cb + c))

    def vec(n=tc):
        return pl.BlockSpec((1, n), lambda b, c: (0, c))

    def full(shape):
        return pl.BlockSpec(shape, lambda b, c: (0,) * len(shape))

    mu_main = mu[:3 * d_rwkv].reshape(1, 3 * d_rwkv)
    mu_tail = jnp.pad(mu[3 * d_rwkv:], (0, n_tail - (mu.shape[0] - 3 * d_rwkv))).reshape(1, n_tail)

    def mu_seg(s):
        return pl.BlockSpec((1, tc), lambda b, c, s=s: (0, s * cb + c))

    def tail_rows(w, start):
        return jnp.pad(w, ((start, n_tail - start - w.shape[0]), (0, 0))).astype(BF16)

    w2p = tail_rows(w2, 0)
    a2p = tail_rows(a2, decay_rank)
    g2p = tail_rows(g2, decay_rank + aaa_rank)
    lowrank = pl.BlockSpec((n_tail, tc), lambda b, c: (0, c))

    out_f32 = jax.ShapeDtypeStruct((bsz, seq, d_rwkv), F32)
    out_spec = pl.BlockSpec((None, seq, tc), lambda b, c: (b, 0, c))
    return pl.pallas_call(
        _even_prep_kernel,
        grid=(bsz, nc),
        in_specs=[
            seg(0), seg(1), seg(2), seg(3), seg(4), seg(5),
            pl.BlockSpec((None, seq, n_tail), lambda b, c: (b, 0, 0)),
            pl.BlockSpec((CONV_W, tc), lambda b, c: (0, c)),
            mu_seg(0), mu_seg(1), mu_seg(2), full((1, n_tail)),
            vec(), lowrank, vec(), lowrank, lowrank, vec(), vec(),
        ],
        out_specs=[out_spec] * 8,
        out_shape=[jax.ShapeDtypeStruct((bsz, seq, d_conv), BF16)] + [out_f32] * 7,
        compiler_params=_params("parallel", "arbitrary"),
        name="even_prep",
    )(pm, pm, pm, pm, pm, pm, pt, conv_w, mu_main, mu_main, mu_main, mu_tail,
      w0.reshape(1, -1), w2p, a0.reshape(1, -1), a2p, g2p,
      k_k.reshape(1, -1), k_a.reshape(1, -1))


def _scan_kernel(r_ref, w_ref, kf_ref, v_ref, kr_ref, lr_ref, rk_ref, lg_ref, lb_ref,
                 y_ref, s_ref, kk_ref, kb_ref, *, steps, n):
    @pl.when(pl.program_id(1) == 0)
    def _():
        s_ref[...] = jnp.zeros_like(s_ref)

    def step(t, carry):
        kr = kr_ref[t]
        nrm = jnp.sqrt(jnp.sum(kr * kr, axis=0, keepdims=True))
        kk = kr / jnp.maximum(nrm, 1e-12)
        kk_ref[...] = kk
        kb_ref[...] = kk * lr_ref[t]
        vt = v_ref[t]

        sa = jnp.zeros_like(vt)
        for k in range(n):
            sa = sa + s_ref[k] * kk_ref[pl.ds(k, 1), :]
        sa = -sa

        y = jnp.zeros_like(vt)
        for k in range(n):
            sk = (s_ref[k] * w_ref[t, pl.ds(k, 1), :] + sa * kb_ref[pl.ds(k, 1), :]
                  + vt * kf_ref[t, pl.ds(k, 1), :])
            s_ref[k] = sk
            y = y + sk * r_ref[t, pl.ds(k, 1), :]

        mean = jnp.mean(y, axis=0, keepdims=True)
        yc = y - mean
        var = jnp.mean(yc * yc, axis=0, keepdims=True)
        yn = yc * lax.rsqrt(var + LNX_EPS)
        bonus = jnp.sum(r_ref[t] * kf_ref[t] * rk_ref[...], axis=0, keepdims=True)
        y_ref[t] = yn * lg_ref[...] + lb_ref[...] + bonus * vt
        return carry

    lax.fori_loop(0, steps, step, 0)


def _scan(r, w, kf, v, kr, lr, rk_t, lg_t, lb_t):
    seq, n, chains = r.shape
    lg = LANES if chains % LANES == 0 else chains
    steps = _pick_tile(seq, (48, 24, 16, 8, 4, 2, 1))
    data = pl.BlockSpec((steps, n, lg), lambda g, t: (t, 0, g))
    tile = pl.BlockSpec((n, lg), lambda g, t: (0, g))
    return pl.pallas_call(
        functools.partial(_scan_kernel, steps=steps, n=n),
        grid=(chains // lg, seq // steps),
        in_specs=[data] * 6 + [tile] * 3,
        out_specs=data,
        out_shape=jax.ShapeDtypeStruct((seq, n, chains), F32),
        scratch_shapes=[pltpu.VMEM((n, n, lg), F32), pltpu.VMEM((n, lg), F32),
                        pltpu.VMEM((n, lg), F32)],
        compiler_params=_params("parallel", "arbitrary"),
        name="rwkv_scan",
    )(r, w, kf, v, kr, lr, rk_t, lg_t, lb_t)


def _to_chains(x, bsz, seq, heads):
    return x.reshape(bsz, seq, heads, HEAD_RWKV).transpose(1, 3, 0, 2).reshape(seq, HEAD_RWKV, bsz * heads)


def _from_chains(y, bsz, seq, heads):
    return y.reshape(seq, HEAD_RWKV, bsz, heads).transpose(2, 0, 3, 1).reshape(bsz, seq, heads * HEAD_RWKV)


def _chain_tile(p, bsz, heads):
    return jnp.tile(p.reshape(heads, HEAD_RWKV).T, (1, bsz))


def _cumsum_kernel(f_ref, b_ref, c_ref):
    z = f_ref[...] + b_ref[...]
    c = jnp.minimum(z, 0.0) - jnp.log(1.0 + jnp.exp(-jnp.abs(z)))
    seq = c.shape[0]
    rows = lax.broadcasted_iota(jnp.int32, c.shape, 0)
    shift = 1
    while shift < seq:
        c = c + jnp.where(rows < shift, 0.0, pltpu.roll(c, shift, 0))
        shift *= 2
    c_ref[...] = c


def _forget_cumsum(p_tail, b_f, bsz, seq):
    n_tail = p_tail.shape[-1]
    b = jnp.pad(b_f, (0, n_tail - b_f.shape[0])).reshape(1, n_tail)
    return pl.pallas_call(
        _cumsum_kernel,
        grid=(bsz,),
        in_specs=[pl.BlockSpec((None, seq, n_tail), lambda i: (i, 0, 0)),
                  pl.BlockSpec((1, n_tail), lambda i: (0, 0))],
        out_specs=pl.BlockSpec((None, seq, n_tail), lambda i: (i, 0, 0)),
        out_shape=jax.ShapeDtypeStruct((bsz, seq, n_tail), F32),
        compiler_params=_params("parallel"),
        name="forget_cumsum",
    )(p_tail.reshape(bsz, seq, n_tail), b)


def _fox_kernel(q_ref, k_ref, v_ref, cc_ref, cr_ref, qg_ref, kg_ref, o_ref,
                qs_ref, ks_ref, vs_ref, *, seq, seq_pad):
    dh = q_ref.shape[-1]
    if seq_pad > seq:
        pad = jnp.zeros((seq_pad - seq, dh), BF16)
        qs_ref[pl.ds(seq, seq_pad - seq), :] = pad
        ks_ref[pl.ds(seq, seq_pad - seq), :] = pad
        vs_ref[pl.ds(seq, seq_pad - seq), :] = pad
    q = q_ref[...]
    k = k_ref[...]
    qs_ref[pl.ds(0, seq), :] = (q * _rms_scale(q) * qg_ref[...] * (dh ** -0.5)).astype(BF16)
    ks_ref[pl.ds(0, seq), :] = (k * _rms_scale(k) * kg_ref[...]).astype(BF16)
    vs_ref[pl.ds(0, seq), :] = v_ref[...].astype(BF16)

    for i in range(seq_pad // Q_BLOCK):
        q0 = i * Q_BLOCK
        q1 = q0 + Q_BLOCK
        qb = qs_ref[pl.ds(q0, Q_BLOCK), :]
        s = lax.dot_general(qb, ks_ref[pl.ds(0, q1), :], (((1,), (1,)), ((), ())),
                            preferred_element_type=F32)
        s = s + cc_ref[pl.ds(q0, Q_BLOCK), :] - cr_ref[:, pl.ds(0, q1)]
        rows = q0 + lax.broadcasted_iota(jnp.int32, s.shape, 0)
        cols = lax.broadcasted_iota(jnp.int32, s.shape, 1)
        s = jnp.where(rows >= cols, s, -jnp.inf)
        m = jnp.max(s, axis=-1, keepdims=True)
        e = jnp.exp(s - m)
        l = jnp.sum(e, axis=-1, keepdims=True)
        ob = jnp.dot(e.astype(BF16), vs_ref[pl.ds(0, q1), :], preferred_element_type=F32) / l
        n_valid = min(q1, seq) - q0
        if n_valid > 0:
            o_ref[pl.ds(q0, n_valid), :] = ob[:n_valid].astype(BF16)


def _fox_attention(p_main, c, q_g, k_g, bsz, seq, heads):
    dh = HEAD_FOX
    seq_pad = -(-seq // Q_BLOCK) * Q_BLOCK
    pm = p_main.reshape(bsz, seq, -1)
    ct = jnp.pad(c[:, :, :heads].transpose(0, 2, 1), ((0, 0), (0, 0), (0, seq_pad - seq)))
    c_col = ct[:, :, :, None]
    c_row = ct[:, :, None, :]

    def seg(s):
        return pl.BlockSpec((None, seq, dh), lambda b, h, s=s: (b, 0, s * heads + h))

    return pl.pallas_call(
        functools.partial(_fox_kernel, seq=seq, seq_pad=seq_pad),
        grid=(bsz, heads),
        in_specs=[
            seg(0), seg(1), seg(2),
            pl.BlockSpec((None, None, seq_pad, 1), lambda b, h: (b, h, 0, 0)),
            pl.BlockSpec((None, None, 1, seq_pad), lambda b, h: (b, h, 0, 0)),
            pl.BlockSpec((1, dh), lambda b, h: (0, 0)),
            pl.BlockSpec((1, dh), lambda b, h: (0, 0)),
        ],
        out_specs=pl.BlockSpec((None, seq, dh), lambda b, h: (b, 0, h)),
        out_shape=jax.ShapeDtypeStruct((bsz, seq, heads * dh), BF16),
        scratch_shapes=[pltpu.VMEM((seq_pad, dh), BF16)] * 3,
        compiler_params=_params("parallel", "parallel"),
        name="fox_attention",
    )(pm, pm, pm, c_col, c_row, q_g.reshape(1, dh), k_g.reshape(1, dh))


def _pad_cols(w, n):
    return jnp.pad(w, ((0, 0), (0, n - w.shape[1])))


def _round_up(n, m):
    return -(-n // m) * m


def kernel(x, meta, norm_g, ffn_in, ffn_out, e_w_in, e_conv_w, e_mu, e_w0, e_w2, e_a0, e_a2,
           e_g2, e_k_k, e_k_a, e_r_k, e_lnx_g, e_lnx_b, e_w_out, o_w_in, o_b_f, o_q_g, o_k_g,
           o_w_out):
    bsz, seq_x, d = x.shape
    depth = norm_g.shape[0]
    seq = seq_x + meta.shape[0]
    d_conv = e_conv_w.shape[-1]
    d_rwkv = e_w0.shape[-1]
    h_rwkv = d_rwkv // HEAD_RWKV
    h_fox = d // HEAD_FOX
    n_even_main = 3 * d_conv + 3 * d_rwkv
    n_odd_main = 4 * d

    h = jnp.concatenate([jnp.broadcast_to(meta[None].astype(x.dtype), (bsz,) + meta.shape), x], axis=1)
    h = h.reshape(bsz * seq, d)

    for l in range(depth):
        g = norm_g[l]
        i = l // 2
        h = _ffn(h, g[0], ffn_in[l, 0].astype(BF16), ffn_out[l, 0].astype(BF16), g[1])
        if l % 2 == 0:
            w_in = e_w_in[i]
            n_tail = _round_up(w_in.shape[1] - n_even_main, LANES)
            p_main, p_tail = _proj(h, g[2], w_in[:, :n_even_main].astype(BF16),
                                   _pad_cols(w_in[:, n_even_main:], n_tail).astype(BF16))
            y_a, r, w, kf, v, kr, lr, gate = _even_prep(
                p_main, p_tail, e_conv_w[i], e_mu[i], e_w0[i], e_w2[i], e_a0[i], e_a2[i], e_g2[i],
                e_k_k[i], e_k_a[i], bsz, seq)
            tc = functools.partial(_to_chains, bsz=bsz, seq=seq, heads=h_rwkv)
            tl = functools.partial(_chain_tile, bsz=bsz, heads=h_rwkv)
            y = _scan(tc(r), tc(w), tc(kf), tc(v), tc(kr), tc(lr),
                      tl(e_r_k[i]), tl(e_lnx_g[i]), tl(e_lnx_b[i]))
            y_b = _from_chains(y, bsz, seq, h_rwkv)
            h = _out_even(h, y_a.reshape(bsz * seq, d_conv), y_b.reshape(bsz * seq, d_rwkv),
                          gate.reshape(bsz * seq, d_rwkv), e_w_out[i].astype(BF16), g[3])
        else:
            w_in = o_w_in[i]
            n_tail = _round_up(w_in.shape[1] - n_odd_main, LANES)
            p_main, p_tail = _proj(h, g[2], w_in[:, :n_odd_main].astype(BF16),
                                   _pad_cols(w_in[:, n_odd_main:], n_tail).astype(BF16))
            c = _forget_cumsum(p_tail, o_b_f[i], bsz, seq)
            o = _fox_attention(p_main, c, o_q_g[i], o_k_g[i], bsz, seq, h_fox)
            h = _out_odd(h, o.reshape(bsz * seq, d), p_main, 3, o_w_out[i].astype(BF16), g[3])
        h = _ffn(h, g[4], ffn_in[l, 1].astype(BF16), ffn_out[l, 1].astype(BF16), g[5])
    return h.reshape(bsz, seq, d)[:, meta.shape[0]:]
```

```python
import functools
import math

import jax
import jax.numpy as jnp
from jax import lax
from jax.experimental import pallas as pl
from jax.experimental.pallas import tpu as pltpu

F32 = jnp.float32
BF16 = jnp.bfloat16

N_META = 16
EPS = 1e-6
LNX_EPS = 64e-5
HEAD_RWKV = 64
HEAD_FOX = 128
CONV_W = 3
Q_BLOCK = 128
FOX_Q_ROWS = 256
LOG2E = math.log2(math.e)

LANES = 128
VMEM_LIMIT = 56 * 1024 * 1024


def _params(*sem):
    return pltpu.CompilerParams(dimension_semantics=sem, vmem_limit_bytes=VMEM_LIMIT)


def _pick_tile(n, candidates):
    for c in candidates:
        if n % c == 0:
            return c
    return n


def _rms_scale(x):
    return lax.rsqrt(jnp.mean(x * x, axis=-1, keepdims=True) + EPS)


def _sigmoid(x):
    return 1.0 / (1.0 + jnp.exp(-x))


def _ffn_kernel(x_ref, gi_ref, wg_ref, wu_ref, wo_ref, go_ref, o_ref, xn_ref):
    k = pl.program_id(1)
    nk = pl.num_programs(1)

    @pl.when(k == 0)
    def _():
        x = x_ref[...]
        xn_ref[...] = (x * _rms_scale(x) * gi_ref[...]).astype(BF16)
        o_ref[...] = jnp.zeros_like(o_ref)

    xn = xn_ref[...]
    gate = jnp.dot(xn, wg_ref[...], preferred_element_type=F32)
    up = jnp.dot(xn, wu_ref[...], preferred_element_type=F32)
    mid = (gate * _sigmoid(gate) * up).astype(BF16)
    o_ref[...] += jnp.dot(mid, wo_ref[...], preferred_element_type=F32)

    @pl.when(k == nk - 1)
    def _():
        y = o_ref[...]
        o_ref[...] = x_ref[...] + 0.5 * (y * _rms_scale(y) * go_ref[...])


def _ffn(h, g_in, w_in, w_out, g_out, layer, slot):
    t, d = h.shape
    d_ff = w_out.shape[2]
    tm = _pick_tile(t, (768, 512, 384, 256, 128, 64, 32, 16))
    tf = _pick_tile(d_ff, (512, 256, 128))
    nk = d_ff // tf
    return pl.pallas_call(
        _ffn_kernel,
        grid=(t // tm, nk),
        in_specs=[
            pl.BlockSpec((tm, d), lambda i, k: (i, 0)),
            pl.BlockSpec((1, d), lambda i, k: (0, 0)),
            pl.BlockSpec((None, None, d, tf), lambda i, k: (layer, slot, 0, k)),
            pl.BlockSpec((None, None, d, tf), lambda i, k: (layer, slot, 0, nk + k)),
            pl.BlockSpec((None, None, tf, d), lambda i, k: (layer, slot, k, 0)),
            pl.BlockSpec((1, d), lambda i, k: (0, 0)),
        ],
        out_specs=pl.BlockSpec((tm, d), lambda i, k: (i, 0)),
        out_shape=jax.ShapeDtypeStruct((t, d), F32),
        scratch_shapes=[pltpu.VMEM((tm, d), BF16)],
        compiler_params=_params("parallel", "arbitrary"),
        name="ffn",
    )(h, g_in.reshape(1, d), w_in, w_in, w_out, g_out.reshape(1, d))


def _proj_kernel(x_ref, g_ref, wm_ref, wt_ref, pm_ref, pt_ref, xn_ref):
    j = pl.program_id(1)

    @pl.when(j == 0)
    def _():
        x = x_ref[...]
        xn = (x * _rms_scale(x) * g_ref[...]).astype(BF16)
        xn_ref[...] = xn
        pt_ref[...] = jnp.dot(xn, wt_ref[...], preferred_element_type=F32)

    pm_ref[...] = jnp.dot(xn_ref[...], wm_ref[...], preferred_element_type=F32)


def _proj(h, g, w_all, idx, n_main, w_tail):
    t, d = h.shape
    n_tail = w_tail.shape[1]
    tm = _pick_tile(t, (768, 512, 384, 256, 128, 64, 32, 16))
    tn = _pick_tile(n_main, (1024, 512, 256, 128))
    return pl.pallas_call(
        _proj_kernel,
        grid=(t // tm, n_main // tn),
        in_specs=[
            pl.BlockSpec((tm, d), lambda i, j: (i, 0)),
            pl.BlockSpec((1, d), lambda i, j: (0, 0)),
            pl.BlockSpec((None, d, tn), lambda i, j: (idx, 0, j)),
            pl.BlockSpec((d, n_tail), lambda i, j: (0, 0)),
        ],
        out_specs=[
            pl.BlockSpec((tm, tn), lambda i, j: (i, j)),
            pl.BlockSpec((tm, n_tail), lambda i, j: (i, 0)),
        ],
        out_shape=[
            jax.ShapeDtypeStruct((t, n_main), F32),
            jax.ShapeDtypeStruct((t, n_tail), F32),
        ],
        scratch_shapes=[pltpu.VMEM((tm, d), BF16)],
        compiler_params=_params("parallel", "arbitrary"),
        name="proj",
    )(h, g.reshape(1, d), w_all, w_tail)


def _out_even_kernel(h_ref, ya_ref, yb_ref, gt_ref, wa_ref, wb_ref, g_ref, o_ref):
    yb = (yb_ref[...] * gt_ref[...]).astype(BF16)
    m = jnp.dot(ya_ref[...], wa_ref[...], preferred_element_type=F32)
    m = m + jnp.dot(yb, wb_ref[...], preferred_element_type=F32)
    o_ref[...] = h_ref[...] + m * _rms_scale(m) * g_ref[...]


def _out_even(h, y_a, y_b, gate, w_out, idx, g):
    t, d = h.shape
    da = y_a.shape[1]
    db = y_b.shape[1]
    tm = _pick_tile(t, (384, 256, 128, 64, 32, 16))
    return pl.pallas_call(
        _out_even_kernel,
        grid=(t // tm,),
        in_specs=[
            pl.BlockSpec((tm, d), lambda i: (i, 0)),
            pl.BlockSpec((tm, da), lambda i: (i, 0)),
            pl.BlockSpec((tm, db), lambda i: (i, 0)),
            pl.BlockSpec((tm, db), lambda i: (i, 0)),
            pl.BlockSpec((None, da, d), lambda i: (idx, 0, 0)),
            pl.BlockSpec((None, db, d), lambda i: (idx, da // db, 0)),
            pl.BlockSpec((1, d), lambda i: (0, 0)),
        ],
        out_specs=pl.BlockSpec((tm, d), lambda i: (i, 0)),
        out_shape=jax.ShapeDtypeStruct((t, d), F32),
        compiler_params=_params("parallel"),
        name="out_even",
    )(h, y_a, y_b, gate, w_out, w_out, g.reshape(1, d))


def _out_odd_kernel(h_ref, o_in_ref, og_ref, w_ref, g_ref, o_ref):
    act = (o_in_ref[...].astype(F32) * _sigmoid(og_ref[...])).astype(BF16)
    m = jnp.dot(act, w_ref[...], preferred_element_type=F32)
    o_ref[...] = h_ref[...] + m * _rms_scale(m) * g_ref[...]


def _out_odd(h, o, p_main, og_block, w_out, idx, g):
    t, d = h.shape
    tm = _pick_tile(t, (384, 256, 128, 64, 32, 16))
    return pl.pallas_call(
        _out_odd_kernel,
        grid=(t // tm,),
        in_specs=[
            pl.BlockSpec((tm, d), lambda i: (i, 0)),
            pl.BlockSpec((tm, d), lambda i: (i, 0)),
            pl.BlockSpec((tm, d), lambda i: (i, og_block)),
            pl.BlockSpec((None, d, d), lambda i: (idx, 0, 0)),
            pl.BlockSpec((1, d), lambda i: (0, 0)),
        ],
        out_specs=pl.BlockSpec((tm, d), lambda i: (i, 0)),
        out_shape=jax.ShapeDtypeStruct((t, d), F32),
        compiler_params=_params("parallel"),
        name="out_odd",
    )(h, o, p_main, w_out, g.reshape(1, d))


def _shift_rows(x, n):
    rows = lax.broadcasted_iota(jnp.int32, x.shape, 0)
    return jnp.where(rows < n, 0.0, pltpu.roll(x, n, 0))


def _even_prep_kernel(gb_ref, gc_ref, hv_ref, pr_ref, pk_ref, pv_ref, pl_ref,
                      cw_ref, mur_ref, muk_ref, muv_ref, mul_ref,
                      w0_ref, w2_ref, a0_ref, a2_ref, g2_ref,
                      ya_ref, r_ref, w_ref, k_ref, v_ref, a_ref, g_ref):
    z = gc_ref[...] * hv_ref[...]
    cw = cw_ref[...]
    conv = cw[0:1] * _shift_rows(z, 2) + cw[1:2] * _shift_rows(z, 1) + cw[2:3] * z
    ya_ref[...] = (gb_ref[...] * conv).astype(BF16)

    def lerp(ref, mu_ref):
        cur = ref[...]
        return cur + (_shift_rows(cur, 1) - cur) * mu_ref[...]

    r = lerp(pr_ref, mur_ref)
    k = lerp(pk_ref, muk_ref)
    v = lerp(pv_ref, muv_ref)
    ul = lerp(pl_ref, mul_ref)
    xw = w0_ref[...] + jnp.dot(jnp.tanh(ul).astype(BF16), w2_ref[...], preferred_element_type=F32)
    w_ref[...] = jnp.exp(-math.exp(-0.5) * _sigmoid(xw))
    a_ref[...] = _sigmoid(a0_ref[...] + jnp.dot(ul.astype(BF16), a2_ref[...],
                                                preferred_element_type=F32))
    g_ref[...] = jnp.dot(_sigmoid(ul).astype(BF16), g2_ref[...], preferred_element_type=F32)
    r_ref[...] = r
    k_ref[...] = k
    v_ref[...] = v


def _even_prep(p_main, p_tail, conv_w, mu, w0, w2, a0, a2, g2, bsz, seq):
    d_conv = conv_w.shape[1]
    d_rwkv = w0.shape[0]
    decay_rank, aaa_rank, gate_rank = w2.shape[0], a2.shape[0], g2.shape[0]
    n_tail = p_tail.shape[-1]
    tc = LANES
    nc = d_rwkv // tc
    assert d_conv == d_rwkv and d_conv % tc == 0
    pm = p_main.reshape(bsz, seq, -1)
    pt = p_tail.reshape(bsz, seq, n_tail)
    cb = d_conv // tc

    def seg(s):
        return pl.BlockSpec((None, seq, tc), lambda b, c, s=s: (b, 0, s * cb + c))

    def vec(n=tc):
        return pl.BlockSpec((1, n), lambda b, c: (0, c))

    def full(shape):
        return pl.BlockSpec(shape, lambda b, c: (0,) * len(shape))

    mu_main = mu[:3 * d_rwkv].reshape(1, 3 * d_rwkv)
    mu_tail = jnp.pad(mu[3 * d_rwkv:], (0, n_tail - (mu.shape[0] - 3 * d_rwkv))).reshape(1, n_tail)

    def mu_seg(s):
        return pl.BlockSpec((1, tc), lambda b, c, s=s: (0, s * cb + c))

    def tail_rows(w, start):
        return jnp.pad(w, ((start, n_tail - start - w.shape[0]), (0, 0))).astype(BF16)

    w2p = tail_rows(w2, 0)
    a2p = tail_rows(a2, decay_rank)
    g2p = tail_rows(g2, decay_rank + aaa_rank)
    lowrank = pl.BlockSpec((n_tail, tc), lambda b, c: (0, c))

    out_f32 = jax.ShapeDtypeStruct((bsz, seq, d_rwkv), F32)
    out_spec = pl.BlockSpec((None, seq, tc), lambda b, c: (b, 0, c))
    return pl.pallas_call(
        _even_prep_kernel,
        grid=(bsz, nc),
        in_specs=[
            seg(0), seg(1), seg(2), seg(3), seg(4), seg(5),
            pl.BlockSpec((None, seq, n_tail), lambda b, c: (b, 0, 0)),
            pl.BlockSpec((CONV_W, tc), lambda b, c: (0, c)),
            mu_seg(0), mu_seg(1), mu_seg(2), full((1, n_tail)),
            vec(), lowrank, vec(), lowrank, lowrank,
        ],
        out_specs=[out_spec] * 7,
        out_shape=[jax.ShapeDtypeStruct((bsz, seq, d_conv), BF16)] + [out_f32] * 6,
        compiler_params=_params("parallel", "arbitrary"),
        name="even_prep",
    )(pm, pm, pm, pm, pm, pm, pt, conv_w, mu_main, mu_main, mu_main, mu_tail,
      w0.reshape(1, -1), w2p, a0.reshape(1, -1), a2p, g2p)


def _scan_kernel(r_ref, w_ref, k_ref, v_ref, lr_ref, kkw_ref, kaw_ref, rk_ref, lg_ref, lb_ref,
                 y_ref, s_ref, kk_ref, kb_ref, kf_ref, *, steps, n):
    @pl.when(pl.program_id(1) == 0)
    def _():
        s_ref[...] = jnp.zeros_like(s_ref)

    def step(t, carry):
        kt = k_ref[t]
        lr = lr_ref[t]
        kr = kt * kkw_ref[...]
        nrm = jnp.sqrt(jnp.sum(kr * kr, axis=0, keepdims=True))
        kk = kr / jnp.maximum(nrm, 1e-12)
        kf = kt * (1.0 + (lr - 1.0) * kaw_ref[...])
        kk_ref[...] = kk
        kb_ref[...] = kk * lr
        kf_ref[...] = kf
        vt = v_ref[t]

        sa = jnp.zeros_like(vt)
        for k in range(n):
            sa = sa + s_ref[k] * kk_ref[pl.ds(k, 1), :]
        sa = -sa

        y = jnp.zeros_like(vt)
        for k in range(n):
            sk = (s_ref[k] * w_ref[t, pl.ds(k, 1), :] + sa * kb_ref[pl.ds(k, 1), :]
                  + vt * kf_ref[pl.ds(k, 1), :])
            s_ref[k] = sk
            y = y + sk * r_ref[t, pl.ds(k, 1), :]

        mean = jnp.mean(y, axis=0, keepdims=True)
        yc = y - mean
        var = jnp.mean(yc * yc, axis=0, keepdims=True)
        yn = yc * lax.rsqrt(var + LNX_EPS)
        bonus = jnp.sum(r_ref[t] * kf * rk_ref[...], axis=0, keepdims=True)
        y_ref[t] = yn * lg_ref[...] + lb_ref[...] + bonus * vt
        return carry

    lax.fori_loop(0, steps, step, 0)


def _scan(r, w, k, v, lr, kk_t, ka_t, rk_t, lg_t, lb_t):
    seq, n, chains = r.shape
    lg = LANES if chains % LANES == 0 else chains
    steps = _pick_tile(seq, (48, 24, 16, 8, 4, 2, 1))
    data = pl.BlockSpec((steps, n, lg), lambda g, t: (t, 0, g))
    tile = pl.BlockSpec((n, lg), lambda g, t: (0, g))
    return pl.pallas_call(
        functools.partial(_scan_kernel, steps=steps, n=n),
        grid=(chains // lg, seq // steps),
        in_specs=[data] * 5 + [tile] * 5,
        out_specs=data,
        out_shape=jax.ShapeDtypeStruct((seq, n, chains), F32),
        scratch_shapes=[pltpu.VMEM((n, n, lg), F32)] + [pltpu.VMEM((n, lg), F32)] * 3,
        compiler_params=_params("parallel", "arbitrary"),
        name="rwkv_scan",
    )(r, w, k, v, lr, kk_t, ka_t, rk_t, lg_t, lb_t)


def _to_chains(x, bsz, seq, heads):
    return x.reshape(bsz, seq, heads, HEAD_RWKV).transpose(1, 3, 0, 2).reshape(seq, HEAD_RWKV, bsz * heads)


def _from_chains(y, bsz, seq, heads):
    return y.reshape(seq, HEAD_RWKV, bsz, heads).transpose(2, 0, 3, 1).reshape(bsz, seq, heads * HEAD_RWKV)


def _chain_tile(p, bsz, heads):
    return jnp.tile(p.reshape(heads, HEAD_RWKV).T, (1, bsz))


def _cumsum_kernel(f_ref, b_ref, c_ref):
    z = f_ref[...] + b_ref[...]
    c = jnp.minimum(z, 0.0) - jnp.log(1.0 + jnp.exp(-jnp.abs(z)))
    seq = c.shape[0]
    rows = lax.broadcasted_iota(jnp.int32, c.shape, 0)
    shift = 1
    while shift < seq:
        c = c + jnp.where(rows < shift, 0.0, pltpu.roll(c, shift, 0))
        shift *= 2
    c_ref[...] = c


def _forget_cumsum(p_tail, b_f, bsz, seq):
    n_tail = p_tail.shape[-1]
    b = jnp.pad(b_f, (0, n_tail - b_f.shape[0])).reshape(1, n_tail)
    return pl.pallas_call(
        _cumsum_kernel,
        grid=(bsz,),
        in_specs=[pl.BlockSpec((None, seq, n_tail), lambda i: (i, 0, 0)),
                  pl.BlockSpec((1, n_tail), lambda i: (0, 0))],
        out_specs=pl.BlockSpec((None, seq, n_tail), lambda i: (i, 0, 0)),
        out_shape=jax.ShapeDtypeStruct((bsz, seq, n_tail), F32),
        compiler_params=_params("parallel"),
        name="forget_cumsum",
    )(p_tail.reshape(bsz, seq, n_tail), b)


def _fox_kernel(q_ref, k_ref, v_ref, c_ref, cr_ref, qg_ref, kg_ref, o_ref,
                qs_ref, ks_ref, vs_ref, cc_ref, *, seq, seq_pad):
    dh = q_ref.shape[-1]
    if seq_pad > seq:
        pad = jnp.zeros((seq_pad - seq, dh), BF16)
        qs_ref[pl.ds(seq, seq_pad - seq), :] = pad
        ks_ref[pl.ds(seq, seq_pad - seq), :] = pad
        vs_ref[pl.ds(seq, seq_pad - seq), :] = pad
        cc_ref[pl.ds(seq, seq_pad - seq), :] = jnp.zeros((seq_pad - seq, 1), F32)
    c_all = c_ref[...]
    lane = lax.broadcasted_iota(jnp.int32, c_all.shape, 1)
    cc_ref[pl.ds(0, seq), :] = jnp.sum(jnp.where(lane == pl.program_id(1), c_all, 0.0),
                                       axis=-1, keepdims=True)
    q = q_ref[...]
    k = k_ref[...]
    qs_ref[pl.ds(0, seq), :] = (q * _rms_scale(q) * qg_ref[...] * (dh ** -0.5 * LOG2E)).astype(BF16)
    ks_ref[pl.ds(0, seq), :] = (k * _rms_scale(k) * kg_ref[...]).astype(BF16)
    vs_ref[pl.ds(0, seq), :] = v_ref[...].astype(BF16)

    def scores(q0, nq):
        return lax.dot_general(qs_ref[pl.ds(q0, nq), :], ks_ref[pl.ds(0, q0 + nq), :],
                               (((1,), (1,)), ((), ())), preferred_element_type=F32)

    chunks = [(q0, min(FOX_Q_ROWS, seq_pad - q0)) for q0 in range(0, seq_pad, FOX_Q_ROWS)]
    qk_next = scores(*chunks[0])
    for ci, (q0, nq) in enumerate(chunks):
        q1 = q0 + nq
        qk = qk_next
        if ci + 1 < len(chunks):
            qk_next = scores(*chunks[ci + 1])
        u = qk - cr_ref[:, pl.ds(0, q1)] * LOG2E
        rows = lax.broadcasted_iota(jnp.int32, (nq, nq), 0)
        cols = lax.broadcasted_iota(jnp.int32, (nq, nq), 1)
        u_diag = jnp.where(rows >= cols, u[:, q0:], -jnp.inf)
        mu = jnp.max(u_diag, axis=-1, keepdims=True)
        if q0 > 0:
            mu = jnp.maximum(mu, jnp.max(u[:, :q0], axis=-1, keepdims=True))
        ct = cc_ref[pl.ds(q0, nq), :] * LOG2E
        shift = ct - (mu + ct)
        e = jnp.exp2(u_diag + shift)
        if q0 > 0:
            e = jnp.concatenate([jnp.exp2(u[:, :q0] + shift), e], axis=1)
        l = jnp.sum(e, axis=-1, keepdims=True)
        ob = jnp.dot(e.astype(BF16), vs_ref[pl.ds(0, q1), :], preferred_element_type=F32) / l
        n_valid = min(q1, seq) - q0
        if n_valid > 0:
            o_ref[pl.ds(q0, n_valid), :] = ob[:n_valid].astype(BF16)


def _fox_attention(p_main, c, q_g, k_g, bsz, seq, heads):
    dh = HEAD_FOX
    seq_pad = -(-seq // Q_BLOCK) * Q_BLOCK
    pm = p_main.reshape(bsz, seq, -1)
    n_tail = c.shape[-1]
    ct = jnp.pad(c[:, :, :heads].transpose(0, 2, 1), ((0, 0), (0, 0), (0, seq_pad - seq)))
    c_row = ct[:, :, None, :]

    def seg(s):
        return pl.BlockSpec((None, seq, dh), lambda b, h, s=s: (b, 0, s * heads + h))

    return pl.pallas_call(
        functools.partial(_fox_kernel, seq=seq, seq_pad=seq_pad),
        grid=(bsz, heads),
        in_specs=[
            seg(0), seg(1), seg(2),
            pl.BlockSpec((None, seq, n_tail), lambda b, h: (b, 0, 0)),
            pl.BlockSpec((None, None, 1, seq_pad), lambda b, h: (b, h, 0, 0)),
            pl.BlockSpec((1, dh), lambda b, h: (0, 0)),
            pl.BlockSpec((1, dh), lambda b, h: (0, 0)),
        ],
        out_specs=pl.BlockSpec((None, seq, dh), lambda b, h: (b, 0, h)),
        out_shape=jax.ShapeDtypeStruct((bsz, seq, heads * dh), BF16),
        scratch_shapes=[pltpu.VMEM((seq_pad, dh), BF16)] * 3 + [pltpu.VMEM((seq_pad, 1), F32)],
        compiler_params=_params("parallel", "parallel"),
        name="fox_attention",
    )(pm, pm, pm, c, c_row, q_g.reshape(1, dh), k_g.reshape(1, dh))


def _pad_cols(w, n):
    return jnp.pad(w, ((0, 0), (0, n - w.shape[1])))


def _round_up(n, m):
    return -(-n // m) * m


def kernel(x, meta, norm_g, ffn_in, ffn_out, e_w_in, e_conv_w, e_mu, e_w0, e_w2, e_a0, e_a2,
           e_g2, e_k_k, e_k_a, e_r_k, e_lnx_g, e_lnx_b, e_w_out, o_w_in, o_b_f, o_q_g, o_k_g,
           o_w_out):
    bsz, seq_x, d = x.shape
    depth = norm_g.shape[0]
    seq = seq_x + meta.shape[0]
    d_conv = e_conv_w.shape[-1]
    d_rwkv = e_w0.shape[-1]
    h_rwkv = d_rwkv // HEAD_RWKV
    h_fox = d // HEAD_FOX
    n_even_main = 3 * d_conv + 3 * d_rwkv
    n_odd_main = 4 * d

    h = jnp.concatenate([jnp.broadcast_to(meta[None].astype(x.dtype), (bsz,) + meta.shape), x], axis=1)
    h = h.reshape(bsz * seq, d)

    ffn_in_b = ffn_in.astype(BF16)
    ffn_out_b = ffn_out.astype(BF16)
    e_w_in_b = e_w_in.astype(BF16)
    e_w_out_b = e_w_out.astype(BF16)
    o_w_in_b = o_w_in.astype(BF16)
    o_w_out_b = o_w_out.astype(BF16)

    def tail_weights(w_in, n_main):
        n_tail = _round_up(w_in.shape[1] - n_main, LANES)
        return _pad_cols(w_in[:, n_main:], n_tail).astype(BF16)

    for l in range(depth):
        g = norm_g[l]
        i = l // 2
        h = _ffn(h, g[0], ffn_in_b, ffn_out_b, g[1], l, 0)
        if l % 2 == 0:
            p_main, p_tail = _proj(h, g[2], e_w_in_b, i, n_even_main,
                                   tail_weights(e_w_in[i], n_even_main))
            y_a, r, w, k, v, lr, gate = _even_prep(
                p_main, p_tail, e_conv_w[i], e_mu[i], e_w0[i], e_w2[i], e_a0[i], e_a2[i], e_g2[i],
                bsz, seq)
            tc = functools.partial(_to_chains, bsz=bsz, seq=seq, heads=h_rwkv)
            tl = functools.partial(_chain_tile, bsz=bsz, heads=h_rwkv)
            y = _scan(tc(r), tc(w), tc(k), tc(v), tc(lr), tl(e_k_k[i]), tl(e_k_a[i]),
                      tl(e_r_k[i]), tl(e_lnx_g[i]), tl(e_lnx_b[i]))
            y_b = _from_chains(y, bsz, seq, h_rwkv)
            h = _out_even(h, y_a.reshape(bsz * seq, d_conv), y_b.reshape(bsz * seq, d_rwkv),
                          gate.reshape(bsz * seq, d_rwkv), e_w_out_b, i, g[3])
        else:
            p_main, p_tail = _proj(h, g[2], o_w_in_b, i, n_odd_main,
                                   tail_weights(o_w_in[i], n_odd_main))
            c = _forget_cumsum(p_tail, o_b_f[i], bsz, seq)
            o = _fox_attention(p_main, c, o_q_g[i], o_k_g[i], bsz, seq, h_fox)
            h = _out_odd(h, o.reshape(bsz * seq, d), p_main, 3, o_w_out_b, i, g[3])
        h = _ffn(h, g[4], ffn_in_b, ffn_out_b, g[5], l, 1)
    return h.reshape(bsz, seq, d)[:, meta.shape[0]:]
```

```python
import functools
import math

import jax
import jax.numpy as jnp
from jax import lax
from jax.experimental import pallas as pl
from jax.experimental.pallas import tpu as pltpu

F32 = jnp.float32
BF16 = jnp.bfloat16

N_META = 16
EPS = 1e-6
LNX_EPS = 64e-5
HEAD_RWKV = 64
HEAD_FOX = 128
CONV_W = 3
Q_BLOCK = 128
FOX_Q_ROWS = 256
LOG2E = math.log2(math.e)

LANES = 128
VMEM_LIMIT = 56 * 1024 * 1024


def _params(*sem):
    return pltpu.CompilerParams(dimension_semantics=sem, vmem_limit_bytes=VMEM_LIMIT)


def _pick_tile(n, candidates):
    for c in candidates:
        if n % c == 0:
            return c
    return n


def _rms_scale(x):
    return lax.rsqrt(jnp.mean(x * x, axis=-1, keepdims=True) + EPS)


def _sigmoid(x):
    return 1.0 / (1.0 + jnp.exp(-x))


SCAN_V_ROWS = 64
N_FFN_IN = 6
N_SCAN_IN = 11


def _ffn_kernel(*refs, scan_plan):
    x_ref, gi_ref, wg_ref, wu_ref, wo_ref, go_ref = refs[:N_FFN_IN]
    if scan_plan is None:
        o_ref, xn_ref = refs[N_FFN_IN:]
    else:
        scan_in = refs[N_FFN_IN:N_FFN_IN + N_SCAN_IN]
        o_ref, y_ref, so_ref, xn_ref, s_ref, kk_ref, kb_ref, kf_ref = refs[N_FFN_IN + N_SCAN_IN:]
        first_steps, steps_per_k = scan_plan

        def run_scan(t0, count):
            for j in range(count):
                _scan_step(t0 + j, *scan_in[:-1], y_ref, s_ref, kk_ref, kb_ref, kf_ref)

    i = pl.program_id(0)
    k = pl.program_id(1)
    nk = pl.num_programs(1)

    @pl.when(k == 0)
    def _():
        x = x_ref[...]
        xn_ref[...] = (x * _rms_scale(x) * gi_ref[...]).astype(BF16)
        o_ref[...] = jnp.zeros_like(o_ref)
        if scan_plan is not None:
            @pl.when(i == 0)
            def _():
                s_ref[...] = scan_in[-1][...]
            run_scan(0, first_steps)

    if scan_plan is not None:
        run_scan(first_steps + k * steps_per_k, steps_per_k)
    xn = xn_ref[...]
    gate = jnp.dot(xn, wg_ref[...], preferred_element_type=F32)
    up = jnp.dot(xn, wu_ref[...], preferred_element_type=F32)
    mid = (gate * _sigmoid(gate) * up).astype(BF16)
    o_ref[...] += jnp.dot(mid, wo_ref[...], preferred_element_type=F32)

    @pl.when(k == nk - 1)
    def _():
        y = o_ref[...]
        o_ref[...] = x_ref[...] + 0.5 * (y * _rms_scale(y) * go_ref[...])
        if scan_plan is not None:
            @pl.when(i == pl.num_programs(0) - 1)
            def _():
                so_ref[...] = s_ref[...]


def _ffn_tiles(t, d_ff, seq=None, parts=None):
    tf = _pick_tile(d_ff, (512, 256, 128))
    nk = d_ff // tf
    if seq is None:
        return _pick_tile(t, (768, 688, 512, 384, 256, 128, 96, 64, 32, 16)), tf, None
    for tm in (384, 688, 768, 512, 256, 128, 96, 64, 32, 16):
        if t % tm == 0 and seq % (parts * (t // tm)) == 0:
            steps = seq // (parts * (t // tm))
            if steps >= nk:
                return tm, tf, (steps - nk * (steps // nk), steps // nk)
    return None


def _ffn(h, g_in, w_in, w_out, g_out, layer, slot, scan=None):
    t, d = h.shape
    d_ff = w_out.shape[2]
    nk_spec = lambda tf: d_ff // tf
    in_specs = lambda tm, tf: [
        pl.BlockSpec((tm, d), lambda i, k: (i, 0)),
        pl.BlockSpec((1, d), lambda i, k: (0, 0)),
        pl.BlockSpec((None, None, d, tf), lambda i, k: (layer, slot, 0, k)),
        pl.BlockSpec((None, None, d, tf), lambda i, k: (layer, slot, 0, nk_spec(tf) + k)),
        pl.BlockSpec((None, None, tf, d), lambda i, k: (layer, slot, k, 0)),
        pl.BlockSpec((1, d), lambda i, k: (0, 0)),
    ]
    args = (h, g_in.reshape(1, d), w_in, w_in, w_out, g_out.reshape(1, d))
    if scan is None:
        tm, tf, _ = _ffn_tiles(t, d_ff)
        return pl.pallas_call(
            functools.partial(_ffn_kernel, scan_plan=None),
            grid=(t // tm, d_ff // tf),
            in_specs=in_specs(tm, tf),
            out_specs=pl.BlockSpec((tm, d), lambda i, k: (i, 0)),
            out_shape=jax.ShapeDtypeStruct((t, d), F32),
            scratch_shapes=[pltpu.VMEM((tm, d), BF16)],
            compiler_params=_params("parallel", "arbitrary"),
            name="ffn",
        )(*args)

    ops, state, part, parts = scan
    seq, n, lanes = ops[0].shape
    tm, tf, plan = _ffn_tiles(t, d_ff, seq, parts)
    n_tiles = t // tm
    steps = seq // (parts * n_tiles)
    data = pl.BlockSpec((steps, n, lanes), lambda i, k: (part * n_tiles + i, 0, 0))
    tile = pl.BlockSpec((n, lanes), lambda i, k: (0, 0))
    st = pl.BlockSpec((n, n, lanes), lambda i, k: (0, 0, 0))
    return pl.pallas_call(
        functools.partial(_ffn_kernel, scan_plan=plan),
        grid=(n_tiles, d_ff // tf),
        in_specs=in_specs(tm, tf) + [data] * 5 + [tile] * 5 + [st],
        out_specs=[pl.BlockSpec((tm, d), lambda i, k: (i, 0)),
                   pl.BlockSpec((steps, n, lanes), lambda i, k: (i, 0, 0)), st],
        out_shape=[jax.ShapeDtypeStruct((t, d), F32),
                   jax.ShapeDtypeStruct((seq // parts, n, lanes), F32),
                   jax.ShapeDtypeStruct((n, n, lanes), F32)],
        scratch_shapes=[pltpu.VMEM((tm, d), BF16), pltpu.VMEM((n, n, lanes), F32)]
        + [pltpu.VMEM((n, lanes), F32)] * 3,
        compiler_params=_params("arbitrary", "arbitrary"),
        name="ffn_scan",
    )(*args, *ops, state)


def _proj_kernel(x_ref, g_ref, wm_ref, wt_ref, pm_ref, pt_ref, xn_ref):
    j = pl.program_id(1)

    @pl.when(j == 0)
    def _():
        x = x_ref[...]
        xn = (x * _rms_scale(x) * g_ref[...]).astype(BF16)
        xn_ref[...] = xn
        pt_ref[...] = jnp.dot(xn, wt_ref[...], preferred_element_type=F32)

    pm_ref[...] = jnp.dot(xn_ref[...], wm_ref[...], preferred_element_type=F32)


def _proj(h, g, w_all, idx, n_main, w_tail):
    t, d = h.shape
    n_tail = w_tail.shape[1]
    tm = _pick_tile(t, (768, 688, 512, 384, 256, 128, 64, 32, 16))
    tn = _pick_tile(n_main, (1024, 512, 256, 128))
    return pl.pallas_call(
        _proj_kernel,
        grid=(t // tm, n_main // tn),
        in_specs=[
            pl.BlockSpec((tm, d), lambda i, j: (i, 0)),
            pl.BlockSpec((1, d), lambda i, j: (0, 0)),
            pl.BlockSpec((None, d, tn), lambda i, j: (idx, 0, j)),
            pl.BlockSpec((d, n_tail), lambda i, j: (0, 0)),
        ],
        out_specs=[
            pl.BlockSpec((tm, tn), lambda i, j: (i, j)),
            pl.BlockSpec((tm, n_tail), lambda i, j: (i, 0)),
        ],
        out_shape=[
            jax.ShapeDtypeStruct((t, n_main), F32),
            jax.ShapeDtypeStruct((t, n_tail), F32),
        ],
        scratch_shapes=[pltpu.VMEM((tm, d), BF16)],
        compiler_params=_params("parallel", "arbitrary"),
        name="proj",
    )(h, g.reshape(1, d), w_all, w_tail)


def _out_even_kernel(h_ref, ya_ref, yb_ref, gt_ref, wa_ref, wb_ref, g_ref, o_ref):
    yb = (yb_ref[...] * gt_ref[...]).astype(BF16)
    m = jnp.dot(ya_ref[...], wa_ref[...], preferred_element_type=F32)
    m = m + jnp.dot(yb, wb_ref[...], preferred_element_type=F32)
    o_ref[...] = h_ref[...] + m * _rms_scale(m) * g_ref[...]


def _out_even(h, y_a, y_b, gate, w_out, idx, g):
    t, d = h.shape
    da = y_a.shape[1]
    db = y_b.shape[1]
    tm = _pick_tile(t, (384, 256, 128, 64, 32, 16))
    return pl.pallas_call(
        _out_even_kernel,
        grid=(t // tm,),
        in_specs=[
            pl.BlockSpec((tm, d), lambda i: (i, 0)),
            pl.BlockSpec((tm, da), lambda i: (i, 0)),
            pl.BlockSpec((tm, db), lambda i: (i, 0)),
            pl.BlockSpec((tm, db), lambda i: (i, 0)),
            pl.BlockSpec((None, da, d), lambda i: (idx, 0, 0)),
            pl.BlockSpec((None, db, d), lambda i: (idx, da // db, 0)),
            pl.BlockSpec((1, d), lambda i: (0, 0)),
        ],
        out_specs=pl.BlockSpec((tm, d), lambda i: (i, 0)),
        out_shape=jax.ShapeDtypeStruct((t, d), F32),
        compiler_params=_params("parallel"),
        name="out_even",
    )(h, y_a, y_b, gate, w_out, w_out, g.reshape(1, d))


def _out_odd_kernel(h_ref, o_in_ref, og_ref, w_ref, g_ref, o_ref):
    act = (o_in_ref[...].astype(F32) * _sigmoid(og_ref[...])).astype(BF16)
    m = jnp.dot(act, w_ref[...], preferred_element_type=F32)
    o_ref[...] = h_ref[...] + m * _rms_scale(m) * g_ref[...]


def _out_odd(h, o, p_main, og_block, w_out, idx, g):
    t, d = h.shape
    tm = _pick_tile(t, (384, 256, 128, 64, 32, 16))
    return pl.pallas_call(
        _out_odd_kernel,
        grid=(t // tm,),
        in_specs=[
            pl.BlockSpec((tm, d), lambda i: (i, 0)),
            pl.BlockSpec((tm, d), lambda i: (i, 0)),
            pl.BlockSpec((tm, d), lambda i: (i, og_block)),
            pl.BlockSpec((None, d, d), lambda i: (idx, 0, 0)),
            pl.BlockSpec((1, d), lambda i: (0, 0)),
        ],
        out_specs=pl.BlockSpec((tm, d), lambda i: (i, 0)),
        out_shape=jax.ShapeDtypeStruct((t, d), F32),
        compiler_params=_params("parallel"),
        name="out_odd",
    )(h, o, p_main, w_out, g.reshape(1, d))


def _shift_rows(x, n):
    rows = lax.broadcasted_iota(jnp.int32, x.shape, 0)
    return jnp.where(rows < n, 0.0, pltpu.roll(x, n, 0))


def _even_prep_kernel(gb_ref, gc_ref, hv_ref, pr_ref, pk_ref, pv_ref, pl_ref,
                      cw_ref, mur_ref, muk_ref, muv_ref, mul_ref,
                      w0_ref, w2_ref, a0_ref, a2_ref, g2_ref,
                      ya_ref, r_ref, w_ref, k_ref, v_ref, a_ref, g_ref):
    z = gc_ref[...] * hv_ref[...]
    cw = cw_ref[...]
    conv = cw[0:1] * _shift_rows(z, 2) + cw[1:2] * _shift_rows(z, 1) + cw[2:3] * z
    ya_ref[...] = (gb_ref[...] * conv).astype(BF16)

    def lerp(ref, mu_ref):
        cur = ref[...]
        return cur + (_shift_rows(cur, 1) - cur) * mu_ref[...]

    r = lerp(pr_ref, mur_ref)
    k = lerp(pk_ref, muk_ref)
    v = lerp(pv_ref, muv_ref)
    ul = lerp(pl_ref, mul_ref)
    xw = w0_ref[...] + jnp.dot(jnp.tanh(ul).astype(BF16), w2_ref[...], preferred_element_type=F32)
    w_ref[...] = jnp.exp(-math.exp(-0.5) * _sigmoid(xw))
    a_ref[...] = _sigmoid(a0_ref[...] + jnp.dot(ul.astype(BF16), a2_ref[...],
                                                preferred_element_type=F32))
    g_ref[...] = jnp.dot(_sigmoid(ul).astype(BF16), g2_ref[...], preferred_element_type=F32)
    r_ref[...] = r
    k_ref[...] = k
    v_ref[...] = v


def _even_prep(p_main, p_tail, conv_w, mu, w0, w2, a0, a2, g2, bsz, seq):
    d_conv = conv_w.shape[1]
    d_rwkv = w0.shape[0]
    decay_rank, aaa_rank, gate_rank = w2.shape[0], a2.shape[0], g2.shape[0]
    n_tail = p_tail.shape[-1]
    tc = LANES
    nc = d_rwkv // tc
    assert d_conv == d_rwkv and d_conv % tc == 0
    pm = p_main.reshape(bsz, seq, -1)
    pt = p_tail.reshape(bsz, seq, n_tail)
    cb = d_conv // tc

    def seg(s):
        return pl.BlockSpec((None, seq, tc), lambda b, c, s=s: (b, 0, s * cb + c))

    def vec(n=tc):
        return pl.BlockSpec((1, n), lambda b, c: (0, c))

    def full(shape):
        return pl.BlockSpec(shape, lambda b, c: (0,) * len(shape))

    mu_main = mu[:3 * d_rwkv].reshape(1, 3 * d_rwkv)
    mu_tail = jnp.pad(mu[3 * d_rwkv:], (0, n_tail - (mu.shape[0] - 3 * d_rwkv))).reshape(1, n_tail)

    def mu_seg(s):
        return pl.BlockSpec((1, tc), lambda b, c, s=s: (0, s * cb + c))

    def tail_rows(w, start):
        return jnp.pad(w, ((start, n_tail - start - w.shape[0]), (0, 0))).astype(BF16)

    w2p = tail_rows(w2, 0)
    a2p = tail_rows(a2, decay_rank)
    g2p = tail_rows(g2, decay_rank + aaa_rank)
    lowrank = pl.BlockSpec((n_tail, tc), lambda b, c: (0, c))

    out_f32 = jax.ShapeDtypeStruct((bsz, seq, d_rwkv), F32)
    out_spec = pl.BlockSpec((None, seq, tc), lambda b, c: (b, 0, c))
    return pl.pallas_call(
        _even_prep_kernel,
        grid=(bsz, nc),
        in_specs=[
            seg(0), seg(1), seg(2), seg(3), seg(4), seg(5),
            pl.BlockSpec((None, seq, n_tail), lambda b, c: (b, 0, 0)),
            pl.BlockSpec((CONV_W, tc), lambda b, c: (0, c)),
            mu_seg(0), mu_seg(1), mu_seg(2), full((1, n_tail)),
            vec(), lowrank, vec(), lowrank, lowrank,
        ],
        out_specs=[out_spec] * 7,
        out_shape=[jax.ShapeDtypeStruct((bsz, seq, d_conv), BF16)] + [out_f32] * 6,
        compiler_params=_params("parallel", "arbitrary"),
        name="even_prep",
    )(pm, pm, pm, pm, pm, pm, pt, conv_w, mu_main, mu_main, mu_main, mu_tail,
      w0.reshape(1, -1), w2p, a0.reshape(1, -1), a2p, g2p)


def _scan_step(t, r_ref, w_ref, k_ref, v_ref, lr_ref, kkw_ref, kaw_ref, rk_ref, lg_ref, lb_ref,
               y_ref, s_ref, kk_ref, kb_ref, kf_ref):
    n = s_ref.shape[0]
    kt = k_ref[t]
    lr = lr_ref[t]
    kr = kt * kkw_ref[...]
    nrm = jnp.sqrt(jnp.sum(kr * kr, axis=0, keepdims=True))
    kk = kr / jnp.maximum(nrm, 1e-12)
    kf = kt * (1.0 + (lr - 1.0) * kaw_ref[...])
    kk_ref[...] = kk
    kb_ref[...] = kk * lr
    kf_ref[...] = kf
    vt = v_ref[t]

    ys = []
    for v0 in range(0, n, SCAN_V_ROWS):
        rows = pl.ds(v0, SCAN_V_ROWS)
        vg = vt[v0:v0 + SCAN_V_ROWS]
        sa = jnp.zeros_like(vg)
        for k in range(n):
            sa = sa + s_ref[k, rows, :] * kk_ref[pl.ds(k, 1), :]
        sa = -sa
        y = jnp.zeros_like(vg)
        for k in range(n):
            sk = (s_ref[k, rows, :] * w_ref[t, pl.ds(k, 1), :] + sa * kb_ref[pl.ds(k, 1), :]
                  + vg * kf_ref[pl.ds(k, 1), :])
            s_ref[k, rows, :] = sk
            y = y + sk * r_ref[t, pl.ds(k, 1), :]
        ys.append(y)
    y = jnp.concatenate(ys, axis=0)

    mean = jnp.mean(y, axis=0, keepdims=True)
    yc = y - mean
    var = jnp.mean(yc * yc, axis=0, keepdims=True)
    yn = yc * lax.rsqrt(var + LNX_EPS)
    bonus = jnp.sum(r_ref[t] * kf * rk_ref[...], axis=0, keepdims=True)
    y_ref[t] = yn * lg_ref[...] + lb_ref[...] + bonus * vt


def _scan_kernel(*refs, steps):
    scan_in = refs[:N_SCAN_IN]
    y_ref, so_ref, s_ref, kk_ref, kb_ref, kf_ref = refs[N_SCAN_IN:]

    @pl.when(pl.program_id(0) == 0)
    def _():
        s_ref[...] = scan_in[-1][...]

    def step(t, carry):
        _scan_step(t, *scan_in[:-1], y_ref, s_ref, kk_ref, kb_ref, kf_ref)
        return carry

    lax.fori_loop(0, steps, step, 0)

    @pl.when(pl.program_id(0) == pl.num_programs(0) - 1)
    def _():
        so_ref[...] = s_ref[...]


def _scan(ops, state, part, parts):
    seq, n, lanes = ops[0].shape
    span = seq // parts
    steps = _pick_tile(span, (48, 43, 24, 16, 8, 4, 2, 1))
    nb = span // steps
    data = pl.BlockSpec((steps, n, lanes), lambda t: (part * nb + t, 0, 0))
    tile = pl.BlockSpec((n, lanes), lambda t: (0, 0))
    st = pl.BlockSpec((n, n, lanes), lambda t: (0, 0, 0))
    return pl.pallas_call(
        functools.partial(_scan_kernel, steps=steps),
        grid=(nb,),
        in_specs=[data] * 5 + [tile] * 5 + [st],
        out_specs=[pl.BlockSpec((steps, n, lanes), lambda t: (t, 0, 0)), st],
        out_shape=[jax.ShapeDtypeStruct((span, n, lanes), F32),
                   jax.ShapeDtypeStruct((n, n, lanes), F32)],
        scratch_shapes=[pltpu.VMEM((n, n, lanes), F32)] + [pltpu.VMEM((n, lanes), F32)] * 3,
        compiler_params=_params("arbitrary"),
        name="rwkv_scan",
    )(*ops, state)


def _to_chains(x, bsz, seq, heads):
    return x.reshape(bsz, seq, heads, HEAD_RWKV).transpose(1, 3, 0, 2).reshape(seq, HEAD_RWKV, bsz * heads)


def _from_chains(y, bsz, seq, heads):
    return y.reshape(seq, HEAD_RWKV, bsz, heads).transpose(2, 0, 3, 1).reshape(bsz, seq, heads * HEAD_RWKV)


def _chain_tile(p, bsz, heads):
    return jnp.tile(p.reshape(heads, HEAD_RWKV).T, (1, bsz))


def _cumsum_kernel(f_ref, b_ref, c_ref):
    z = f_ref[...] + b_ref[...]
    c = jnp.minimum(z, 0.0) - jnp.log(1.0 + jnp.exp(-jnp.abs(z)))
    seq = c.shape[0]
    rows = lax.broadcasted_iota(jnp.int32, c.shape, 0)
    shift = 1
    while shift < seq:
        c = c + jnp.where(rows < shift, 0.0, pltpu.roll(c, shift, 0))
        shift *= 2
    c_ref[...] = c


def _forget_cumsum(p_tail, b_f, bsz, seq):
    n_tail = p_tail.shape[-1]
    b = jnp.pad(b_f, (0, n_tail - b_f.shape[0])).reshape(1, n_tail)
    return pl.pallas_call(
        _cumsum_kernel,
        grid=(bsz,),
        in_specs=[pl.BlockSpec((None, seq, n_tail), lambda i: (i, 0, 0)),
                  pl.BlockSpec((1, n_tail), lambda i: (0, 0))],
        out_specs=pl.BlockSpec((None, seq, n_tail), lambda i: (i, 0, 0)),
        out_shape=jax.ShapeDtypeStruct((bsz, seq, n_tail), F32),
        compiler_params=_params("parallel"),
        name="forget_cumsum",
    )(p_tail.reshape(bsz, seq, n_tail), b)


def _fox_kernel(q_ref, k_ref, v_ref, c_ref, cr_ref, qg_ref, kg_ref, o_ref,
                qs_ref, ks_ref, vs_ref, cc_ref, *, seq, seq_pad):
    dh = q_ref.shape[-1]
    if seq_pad > seq:
        pad = jnp.zeros((seq_pad - seq, dh), BF16)
        qs_ref[pl.ds(seq, seq_pad - seq), :] = pad
        ks_ref[pl.ds(seq, seq_pad - seq), :] = pad
        vs_ref[pl.ds(seq, seq_pad - seq), :] = pad
        cc_ref[pl.ds(seq, seq_pad - seq), :] = jnp.zeros((seq_pad - seq, 1), F32)
    c_all = c_ref[...]
    lane = lax.broadcasted_iota(jnp.int32, c_all.shape, 1)
    cc_ref[pl.ds(0, seq), :] = jnp.sum(jnp.where(lane == pl.program_id(1), c_all, 0.0),
                                       axis=-1, keepdims=True)
    q = q_ref[...]
    k = k_ref[...]
    qs_ref[pl.ds(0, seq), :] = (q * _rms_scale(q) * qg_ref[...] * (dh ** -0.5 * LOG2E)).astype(BF16)
    ks_ref[pl.ds(0, seq), :] = (k * _rms_scale(k) * kg_ref[...]).astype(BF16)
    vs_ref[pl.ds(0, seq), :] = v_ref[...].astype(BF16)

    def scores(q0, nq):
        return lax.dot_general(qs_ref[pl.ds(q0, nq), :], ks_ref[pl.ds(0, q0 + nq), :],
                               (((1,), (1,)), ((), ())), preferred_element_type=F32)

    chunks = [(q0, min(FOX_Q_ROWS, seq_pad - q0)) for q0 in range(0, seq_pad, FOX_Q_ROWS)]
    qk_next = scores(*chunks[0])
    for ci, (q0, nq) in enumerate(chunks):
        q1 = q0 + nq
        qk = qk_next
        if ci + 1 < len(chunks):
            qk_next = scores(*chunks[ci + 1])
        u = qk - cr_ref[:, pl.ds(0, q1)] * LOG2E
        rows = lax.broadcasted_iota(jnp.int32, (nq, nq), 0)
        cols = lax.broadcasted_iota(jnp.int32, (nq, nq), 1)
        u_diag = jnp.where(rows >= cols, u[:, q0:], -jnp.inf)
        mu = jnp.max(u_diag, axis=-1, keepdims=True)
        if q0 > 0:
            mu = jnp.maximum(mu, jnp.max(u[:, :q0], axis=-1, keepdims=True))
        ct = cc_ref[pl.ds(q0, nq), :] * LOG2E
        shift = ct - (mu + ct)
        e = jnp.exp2(u_diag + shift)
        if q0 > 0:
            e = jnp.concatenate([jnp.exp2(u[:, :q0] + shift), e], axis=1)
        l = jnp.sum(e, axis=-1, keepdims=True)
        ob = jnp.dot(e.astype(BF16), vs_ref[pl.ds(0, q1), :], preferred_element_type=F32) / l
        n_valid = min(q1, seq) - q0
        if n_valid > 0:
            o_ref[pl.ds(q0, n_valid), :] = ob[:n_valid].astype(BF16)


def _fox_attention(p_main, c, q_g, k_g, bsz, seq, heads):
    dh = HEAD_FOX
    seq_pad = -(-seq // Q_BLOCK) * Q_BLOCK
    pm = p_main.reshape(bsz, seq, -1)
    n_tail = c.shape[-1]
    ct = jnp.pad(c[:, :, :heads].transpose(0, 2, 1), ((0, 0), (0, 0), (0, seq_pad - seq)))
    c_row = ct[:, :, None, :]

    def seg(s):
        return pl.BlockSpec((None, seq, dh), lambda b, h, s=s: (b, 0, s * heads + h))

    return pl.pallas_call(
        functools.partial(_fox_kernel, seq=seq, seq_pad=seq_pad),
        grid=(bsz, heads),
        in_specs=[
            seg(0), seg(1), seg(2),
            pl.BlockSpec((None, seq, n_tail), lambda b, h: (b, 0, 0)),
            pl.BlockSpec((None, None, 1, seq_pad), lambda b, h: (b, h, 0, 0)),
            pl.BlockSpec((1, dh), lambda b, h: (0, 0)),
            pl.BlockSpec((1, dh), lambda b, h: (0, 0)),
        ],
        out_specs=pl.BlockSpec((None, seq, dh), lambda b, h: (b, 0, h)),
        out_shape=jax.ShapeDtypeStruct((bsz, seq, heads * dh), BF16),
        scratch_shapes=[pltpu.VMEM((seq_pad, dh), BF16)] * 3 + [pltpu.VMEM((seq_pad, 1), F32)],
        compiler_params=_params("parallel", "parallel"),
        name="fox_attention",
    )(pm, pm, pm, c, c_row, q_g.reshape(1, dh), k_g.reshape(1, dh))


def _pad_cols(w, n):
    return jnp.pad(w, ((0, 0), (0, n - w.shape[1])))


def _round_up(n, m):
    return -(-n // m) * m


def kernel(x, meta, norm_g, ffn_in, ffn_out, e_w_in, e_conv_w, e_mu, e_w0, e_w2, e_a0, e_a2,
           e_g2, e_k_k, e_k_a, e_r_k, e_lnx_g, e_lnx_b, e_w_out, o_w_in, o_b_f, o_q_g, o_k_g,
           o_w_out):
    bsz, seq_x, d = x.shape
    depth = norm_g.shape[0]
    seq = seq_x + meta.shape[0]
    d_conv = e_conv_w.shape[-1]
    d_rwkv = e_w0.shape[-1]
    h_rwkv = d_rwkv // HEAD_RWKV
    h_fox = d // HEAD_FOX
    n_even_main = 3 * d_conv + 3 * d_rwkv
    n_odd_main = 4 * d

    h = jnp.concatenate([jnp.broadcast_to(meta[None].astype(x.dtype), (bsz,) + meta.shape), x], axis=1)

    ffn_in_b = ffn_in.astype(BF16)
    ffn_out_b = ffn_out.astype(BF16)
    e_w_in_b = e_w_in.astype(BF16)
    e_w_out_b = e_w_out.astype(BF16)
    o_w_in_b = o_w_in.astype(BF16)
    o_w_out_b = o_w_out.astype(BF16)

    def tail_weights(w_in, n_main):
        n_tail = _round_up(w_in.shape[1] - n_main, LANES)
        return _pad_cols(w_in[:, n_main:], n_tail).astype(BF16)

    n_streams = 2 if bsz % 2 == 0 else 1
    sb = bsz // n_streams
    chains = sb * h_rwkv
    scan_parts = 2

    def stream(h):
        h = h.reshape(sb * seq, d)
        for l in range(depth):
            g = norm_g[l]
            i = l // 2
            h = yield ("ffn", h, g[0], g[1], l, 0)
            if l % 2 == 0:
                p_main, p_tail = _proj(h, g[2], e_w_in_b, i, n_even_main,
                                       tail_weights(e_w_in[i], n_even_main))
                y_a, r, w, k, v, lr, gate = _even_prep(
                    p_main, p_tail, e_conv_w[i], e_mu[i], e_w0[i], e_w2[i], e_a0[i], e_a2[i],
                    e_g2[i], sb, seq)
                tc = functools.partial(_to_chains, bsz=sb, seq=seq, heads=h_rwkv)
                tl = functools.partial(_chain_tile, bsz=sb, heads=h_rwkv)
                y = yield ("scan", (tc(r), tc(w), tc(k), tc(v), tc(lr), tl(e_k_k[i]), tl(e_k_a[i]),
                                    tl(e_r_k[i]), tl(e_lnx_g[i]), tl(e_lnx_b[i])))
                y_b = _from_chains(y, sb, seq, h_rwkv)
                h = _out_even(h, y_a.reshape(sb * seq, d_conv), y_b.reshape(sb * seq, d_rwkv),
                              gate.reshape(sb * seq, d_rwkv), e_w_out_b, i, g[3])
            else:
                p_main, p_tail = _proj(h, g[2], o_w_in_b, i, n_odd_main,
                                       tail_weights(o_w_in[i], n_odd_main))
                c = _forget_cumsum(p_tail, o_b_f[i], sb, seq)
                o = _fox_attention(p_main, c, o_q_g[i], o_k_g[i], sb, seq, h_fox)
                h = _out_odd(h, o.reshape(sb * seq, d), p_main, 3, o_w_out_b, i, g[3])
            h = yield ("ffn", h, g[4], g[5], l, 1)
        return h.reshape(sb, seq, d)

    class Pending:
        def __init__(self, ops):
            self.ops, self.ys = ops, []
            self.state = jnp.zeros((HEAD_RWKV, HEAD_RWKV, chains), F32)

        def done(self):
            return len(self.ys) == scan_parts

    fusable = _ffn_tiles(sb * seq, ffn_out.shape[2], seq, scan_parts) is not None
    gens = [stream(h[s * sb:(s + 1) * sb]) for s in range(n_streams)]
    reqs = [next(gen) for gen in gens]
    pending = [None] * n_streams
    results = [None] * n_streams

    def advance(s, value):
        try:
            reqs[s] = gens[s].send(value)
        except StopIteration as stop:
            reqs[s], results[s] = None, stop.value

    cur = 0
    while any(r is not None for r in reqs):
        if reqs[cur] is None or pending[cur] is not None:
            cur = (cur + 1) % n_streams
        other = (cur + 1) % n_streams
        req = reqs[cur]
        if req is None or pending[cur] is not None:
            s = cur if pending[cur] is not None else other
            p = pending[s]
            y_part, p.state = _scan(p.ops, p.state, len(p.ys), scan_parts)
            p.ys.append(y_part)
        elif req[0] == "scan":
            pending[cur] = Pending(req[1])
            continue
        else:
            _, hh, g_in, g_out, l, slot = req
            p = pending[other] if n_streams > 1 else None
            if p is not None and fusable:
                hh, y_part, p.state = _ffn(hh, g_in, ffn_in_b, ffn_out_b, g_out, l, slot,
                                           scan=(p.ops, p.state, len(p.ys), scan_parts))
                p.ys.append(y_part)
            else:
                hh = _ffn(hh, g_in, ffn_in_b, ffn_out_b, g_out, l, slot)
            advance(cur, hh)
        for s in range(n_streams):
            if pending[s] is not None and pending[s].done():
                y = jnp.concatenate(pending[s].ys, axis=0)
                pending[s] = None
                advance(s, y)
    out = jnp.concatenate(results, axis=0) if n_streams > 1 else results[0]
    return out[:, meta.shape[0]:]
```

```python
import functools
import math

import jax
import jax.numpy as jnp
from jax import lax
from jax.experimental import pallas as pl
from jax.experimental.pallas import tpu as pltpu

F32 = jnp.float32
BF16 = jnp.bfloat16

N_META = 16
EPS = 1e-6
LNX_EPS = 64e-5
HEAD_RWKV = 64
HEAD_FOX = 128
CONV_W = 3
Q_BLOCK = 128
FOX_Q_ROWS = 256
LOG2E = math.log2(math.e)

LANES = 128
VMEM_LIMIT = 56 * 1024 * 1024


def _params(*sem):
    return pltpu.CompilerParams(dimension_semantics=sem, vmem_limit_bytes=VMEM_LIMIT)


def _pick_tile(n, candidates):
    for c in candidates:
        if n % c == 0:
            return c
    return n


def _rms_scale(x):
    return lax.rsqrt(jnp.mean(x * x, axis=-1, keepdims=True) + EPS)


def _sigmoid(x):
    return 1.0 / (1.0 + jnp.exp(-x))


def _ffn_kernel(x_ref, gi_ref, wg_ref, wu_ref, wo_ref, go_ref, o_ref, xn_ref):
    k = pl.program_id(1)
    nk = pl.num_programs(1)

    @pl.when(k == 0)
    def _():
        x = x_ref[...]
        xn_ref[...] = (x * _rms_scale(x) * gi_ref[...]).astype(BF16)
        o_ref[...] = jnp.zeros_like(o_ref)

    xn = xn_ref[...]
    gate = jnp.dot(xn, wg_ref[...], preferred_element_type=F32)
    up = jnp.dot(xn, wu_ref[...], preferred_element_type=F32)
    mid = (gate * _sigmoid(gate) * up).astype(BF16)
    o_ref[...] += jnp.dot(mid, wo_ref[...], preferred_element_type=F32)

    @pl.when(k == nk - 1)
    def _():
        y = o_ref[...]
        o_ref[...] = x_ref[...] + 0.5 * (y * _rms_scale(y) * go_ref[...])


def _ffn(h, g_in, w_in, w_out, g_out, layer, slot):
    t, d = h.shape
    d_ff = w_out.shape[2]
    tm = _pick_tile(t, (768, 512, 384, 256, 128, 64, 32, 16))
    tf = _pick_tile(d_ff, (512, 256, 128))
    nk = d_ff // tf
    return pl.pallas_call(
        _ffn_kernel,
        grid=(t // tm, nk),
        in_specs=[
            pl.BlockSpec((tm, d), lambda i, k: (i, 0)),
            pl.BlockSpec((1, d), lambda i, k: (0, 0)),
            pl.BlockSpec((None, None, d, tf), lambda i, k: (layer, slot, 0, k)),
            pl.BlockSpec((None, None, d, tf), lambda i, k: (layer, slot, 0, nk + k)),
            pl.BlockSpec((None, None, tf, d), lambda i, k: (layer, slot, k, 0)),
            pl.BlockSpec((1, d), lambda i, k: (0, 0)),
        ],
        out_specs=pl.BlockSpec((tm, d), lambda i, k: (i, 0)),
        out_shape=jax.ShapeDtypeStruct((t, d), F32),
        scratch_shapes=[pltpu.VMEM((tm, d), BF16)],
        compiler_params=_params("parallel", "arbitrary"),
        name="ffn",
    )(h, g_in.reshape(1, d), w_in, w_in, w_out, g_out.reshape(1, d))


def _proj_kernel(x_ref, g_ref, wm_ref, wt_ref, pm_ref, pt_ref, xn_ref):
    j = pl.program_id(1)

    @pl.when(j == 0)
    def _():
        x = x_ref[...]
        xn = (x * _rms_scale(x) * g_ref[...]).astype(BF16)
        xn_ref[...] = xn
        pt_ref[...] = jnp.dot(xn, wt_ref[...], preferred_element_type=F32)

    pm_ref[...] = jnp.dot(xn_ref[...], wm_ref[...], preferred_element_type=F32)


def _proj(h, g, w_all, idx, n_main, w_tail):
    t, d = h.shape
    n_tail = w_tail.shape[1]
    tm = _pick_tile(t, (768, 512, 384, 256, 128, 64, 32, 16))
    tn = _pick_tile(n_main, (1024, 512, 256, 128))
    return pl.pallas_call(
        _proj_kernel,
        grid=(t // tm, n_main // tn),
        in_specs=[
            pl.BlockSpec((tm, d), lambda i, j: (i, 0)),
            pl.BlockSpec((1, d), lambda i, j: (0, 0)),
            pl.BlockSpec((None, d, tn), lambda i, j: (idx, 0, j)),
            pl.BlockSpec((d, n_tail), lambda i, j: (0, 0)),
        ],
        out_specs=[
            pl.BlockSpec((tm, tn), lambda i, j: (i, j)),
            pl.BlockSpec((tm, n_tail), lambda i, j: (i, 0)),
        ],
        out_shape=[
            jax.ShapeDtypeStruct((t, n_main), F32),
            jax.ShapeDtypeStruct((t, n_tail), F32),
        ],
        scratch_shapes=[pltpu.VMEM((tm, d), BF16)],
        compiler_params=_params("parallel", "arbitrary"),
        name="proj",
    )(h, g.reshape(1, d), w_all, w_tail)


def _proj_odd_kernel(x_ref, g_ref, wm_ref, wt_ref, qg_ref, kg_ref, qkv_ref, og_ref, pt_ref,
                     xn_ref, *, seg_blocks):
    j = pl.program_id(1)
    dh = qg_ref.shape[-1]

    @pl.when(j == 0)
    def _():
        x = x_ref[...]
        xn = (x * _rms_scale(x) * g_ref[...]).astype(BF16)
        xn_ref[...] = xn
        pt_ref[...] = jnp.dot(xn, wt_ref[...], preferred_element_type=F32)

    def project():
        return jnp.dot(xn_ref[...], wm_ref[...], preferred_element_type=F32)

    def head_norm(acc, gain, scale):
        heads = [acc[:, c:c + dh] for c in range(0, acc.shape[1], dh)]
        return jnp.concatenate([(xh * _rms_scale(xh) * gain * scale).astype(BF16) for xh in heads],
                               axis=1)

    @pl.when(j < seg_blocks)
    def _():
        qkv_ref[...] = head_norm(project(), qg_ref[...], dh ** -0.5 * LOG2E)

    @pl.when((j >= seg_blocks) & (j < 2 * seg_blocks))
    def _():
        qkv_ref[...] = head_norm(project(), kg_ref[...], 1.0)

    @pl.when((j >= 2 * seg_blocks) & (j < 3 * seg_blocks))
    def _():
        qkv_ref[...] = project().astype(BF16)

    @pl.when(j >= 3 * seg_blocks)
    def _():
        og_ref[...] = project()


def _proj_odd(h, g, w_all, idx, w_tail, q_g, k_g):
    t, d = h.shape
    n_tail = w_tail.shape[1]
    dh = q_g.shape[0]
    tm = _pick_tile(t, (768, 512, 384, 256, 128, 64, 32, 16))
    tn = _pick_tile(d, (1024, 512, 256, 128))
    sb = d // tn
    return pl.pallas_call(
        functools.partial(_proj_odd_kernel, seg_blocks=sb),
        grid=(t // tm, 4 * sb),
        in_specs=[
            pl.BlockSpec((tm, d), lambda i, j: (i, 0)),
            pl.BlockSpec((1, d), lambda i, j: (0, 0)),
            pl.BlockSpec((None, d, tn), lambda i, j: (idx, 0, j)),
            pl.BlockSpec((d, n_tail), lambda i, j: (0, 0)),
            pl.BlockSpec((1, dh), lambda i, j: (0, 0)),
            pl.BlockSpec((1, dh), lambda i, j: (0, 0)),
        ],
        out_specs=[
            pl.BlockSpec((tm, tn), lambda i, j: (i, jnp.minimum(j, 3 * sb - 1))),
            pl.BlockSpec((tm, tn), lambda i, j: (i, jnp.maximum(j - 3 * sb, 0))),
            pl.BlockSpec((tm, n_tail), lambda i, j: (i, 0)),
        ],
        out_shape=[
            jax.ShapeDtypeStruct((t, 3 * d), BF16),
            jax.ShapeDtypeStruct((t, d), F32),
            jax.ShapeDtypeStruct((t, n_tail), F32),
        ],
        scratch_shapes=[pltpu.VMEM((tm, d), BF16)],
        compiler_params=_params("parallel", "arbitrary"),
        name="proj_odd",
    )(h, g.reshape(1, d), w_all, w_tail, q_g.reshape(1, dh), k_g.reshape(1, dh))


def _out_even_kernel(h_ref, ya_ref, yb_ref, gt_ref, wa_ref, wb_ref, g_ref, o_ref):
    yb = (yb_ref[...].astype(F32) * gt_ref[...]).astype(BF16)
    m = jnp.dot(ya_ref[...], wa_ref[...], preferred_element_type=F32)
    m = m + jnp.dot(yb, wb_ref[...], preferred_element_type=F32)
    o_ref[...] = h_ref[...] + m * _rms_scale(m) * g_ref[...]


def _out_even(h, y_a, y_b, gate, w_out, idx, g):
    t, d = h.shape
    da = y_a.shape[1]
    db = y_b.shape[1]
    tm = _pick_tile(t, (384, 256, 128, 64, 32, 16))
    return pl.pallas_call(
        _out_even_kernel,
        grid=(t // tm,),
        in_specs=[
            pl.BlockSpec((tm, d), lambda i: (i, 0)),
            pl.BlockSpec((tm, da), lambda i: (i, 0)),
            pl.BlockSpec((tm, db), lambda i: (i, 0)),
            pl.BlockSpec((tm, db), lambda i: (i, 0)),
            pl.BlockSpec((None, da, d), lambda i: (idx, 0, 0)),
            pl.BlockSpec((None, db, d), lambda i: (idx, da // db, 0)),
            pl.BlockSpec((1, d), lambda i: (0, 0)),
        ],
        out_specs=pl.BlockSpec((tm, d), lambda i: (i, 0)),
        out_shape=jax.ShapeDtypeStruct((t, d), F32),
        compiler_params=_params("parallel"),
        name="out_even",
    )(h, y_a, y_b, gate, w_out, w_out, g.reshape(1, d))


def _out_odd_kernel(h_ref, o_in_ref, og_ref, w_ref, g_ref, o_ref):
    act = (o_in_ref[...].astype(F32) * _sigmoid(og_ref[...])).astype(BF16)
    m = jnp.dot(act, w_ref[...], preferred_element_type=F32)
    o_ref[...] = h_ref[...] + m * _rms_scale(m) * g_ref[...]


def _out_odd(h, o, og, w_out, idx, g):
    t, d = h.shape
    tm = _pick_tile(t, (384, 256, 128, 64, 32, 16))
    return pl.pallas_call(
        _out_odd_kernel,
        grid=(t // tm,),
        in_specs=[
            pl.BlockSpec((tm, d), lambda i: (i, 0)),
            pl.BlockSpec((tm, d), lambda i: (i, 0)),
            pl.BlockSpec((tm, d), lambda i: (i, 0)),
            pl.BlockSpec((None, d, d), lambda i: (idx, 0, 0)),
            pl.BlockSpec((1, d), lambda i: (0, 0)),
        ],
        out_specs=pl.BlockSpec((tm, d), lambda i: (i, 0)),
        out_shape=jax.ShapeDtypeStruct((t, d), F32),
        compiler_params=_params("parallel"),
        name="out_odd",
    )(h, o, og, w_out, g.reshape(1, d))


def _shift_rows(x, n):
    rows = lax.broadcasted_iota(jnp.int32, x.shape, 0)
    return jnp.where(rows < n, 0.0, pltpu.roll(x, n, 0))


def _even_prep_kernel(gb_ref, gc_ref, hv_ref, pr_ref, pk_ref, pv_ref, pl_ref,
                      cw_ref, mur_ref, muk_ref, muv_ref, mul_ref,
                      w0_ref, w2_ref, a0_ref, a2_ref, g2_ref,
                      ya_ref, r_ref, w_ref, k_ref, v_ref, a_ref, g_ref):
    z = gc_ref[...] * hv_ref[...]
    cw = cw_ref[...]
    conv = cw[0:1] * _shift_rows(z, 2) + cw[1:2] * _shift_rows(z, 1) + cw[2:3] * z
    ya_ref[...] = (gb_ref[...] * conv).astype(BF16)

    def lerp(ref, mu_ref):
        cur = ref[...]
        return cur + (_shift_rows(cur, 1) - cur) * mu_ref[...]

    r = lerp(pr_ref, mur_ref)
    k = lerp(pk_ref, muk_ref)
    v = lerp(pv_ref, muv_ref)
    ul = lerp(pl_ref, mul_ref)
    xw = w0_ref[...] + jnp.dot(jnp.tanh(ul).astype(BF16), w2_ref[...], preferred_element_type=F32)
    w_ref[...] = jnp.exp(-math.exp(-0.5) * _sigmoid(xw))
    a_ref[...] = _sigmoid(a0_ref[...] + jnp.dot(ul.astype(BF16), a2_ref[...],
                                                preferred_element_type=F32))
    g_ref[...] = jnp.dot(_sigmoid(ul).astype(BF16), g2_ref[...], preferred_element_type=F32)
    r_ref[...] = r
    k_ref[...] = k
    v_ref[...] = v


def _even_prep(p_main, p_tail, conv_w, mu, w0, w2, a0, a2, g2, bsz, seq):
    d_conv = conv_w.shape[1]
    d_rwkv = w0.shape[0]
    decay_rank, aaa_rank, gate_rank = w2.shape[0], a2.shape[0], g2.shape[0]
    n_tail = p_tail.shape[-1]
    tc = LANES
    nc = d_rwkv // tc
    assert d_conv == d_rwkv and d_conv % tc == 0
    pm = p_main.reshape(bsz, seq, -1)
    pt = p_tail.reshape(bsz, seq, n_tail)
    cb = d_conv // tc

    def seg(s):
        return pl.BlockSpec((None, seq, tc), lambda b, c, s=s: (b, 0, s * cb + c))

    def vec(n=tc):
        return pl.BlockSpec((1, n), lambda b, c: (0, c))

    def full(shape):
        return pl.BlockSpec(shape, lambda b, c: (0,) * len(shape))

    mu_main = mu[:3 * d_rwkv].reshape(1, 3 * d_rwkv)
    mu_tail = jnp.pad(mu[3 * d_rwkv:], (0, n_tail - (mu.shape[0] - 3 * d_rwkv))).reshape(1, n_tail)

    def mu_seg(s):
        return pl.BlockSpec((1, tc), lambda b, c, s=s: (0, s * cb + c))

    def tail_rows(w, start):
        return jnp.pad(w, ((start, n_tail - start - w.shape[0]), (0, 0))).astype(BF16)

    w2p = tail_rows(w2, 0)
    a2p = tail_rows(a2, decay_rank)
    g2p = tail_rows(g2, decay_rank + aaa_rank)
    lowrank = pl.BlockSpec((n_tail, tc), lambda b, c: (0, c))

    out_f32 = jax.ShapeDtypeStruct((bsz, seq, d_rwkv), F32)
    out_spec = pl.BlockSpec((None, seq, tc), lambda b, c: (b, 0, c))
    return pl.pallas_call(
        _even_prep_kernel,
        grid=(bsz, nc),
        in_specs=[
            seg(0), seg(1), seg(2), seg(3), seg(4), seg(5),
            pl.BlockSpec((None, seq, n_tail), lambda b, c: (b, 0, 0)),
            pl.BlockSpec((CONV_W, tc), lambda b, c: (0, c)),
            mu_seg(0), mu_seg(1), mu_seg(2), full((1, n_tail)),
            vec(), lowrank, vec(), lowrank, lowrank,
        ],
        out_specs=[out_spec] * 7,
        out_shape=[jax.ShapeDtypeStruct((bsz, seq, d_conv), BF16)] + [out_f32] * 6,
        compiler_params=_params("parallel", "arbitrary"),
        name="even_prep",
    )(pm, pm, pm, pm, pm, pm, pt, conv_w, mu_main, mu_main, mu_main, mu_tail,
      w0.reshape(1, -1), w2p, a0.reshape(1, -1), a2p, g2p)


def _scan_kernel(r_ref, w_ref, k_ref, v_ref, lr_ref, kkw_ref, kaw_ref, rk_ref, lg_ref, lb_ref,
                 y_ref, s_ref, kk_ref, kb_ref, kf_ref, *, steps, n):
    @pl.when(pl.program_id(1) == 0)
    def _():
        s_ref[...] = jnp.zeros_like(s_ref)

    def step(t, carry):
        kt = k_ref[t]
        lr = lr_ref[t]
        kr = kt * kkw_ref[...]
        nrm = jnp.sqrt(jnp.sum(kr * kr, axis=0, keepdims=True))
        kk = kr / jnp.maximum(nrm, 1e-12)
        kf = kt * (1.0 + (lr - 1.0) * kaw_ref[...])
        kk_ref[...] = kk
        kb_ref[...] = kk * lr
        kf_ref[...] = kf
        vt = v_ref[t]

        sa = jnp.zeros_like(vt)
        for k in range(n):
            sa = sa + s_ref[k] * kk_ref[pl.ds(k, 1), :]
        sa = -sa

        y = jnp.zeros_like(vt)
        for k in range(n):
            sk = (s_ref[k] * w_ref[t, pl.ds(k, 1), :] + sa * kb_ref[pl.ds(k, 1), :]
                  + vt * kf_ref[pl.ds(k, 1), :])
            s_ref[k] = sk
            y = y + sk * r_ref[t, pl.ds(k, 1), :]

        mean = jnp.mean(y, axis=0, keepdims=True)
        yc = y - mean
        var = jnp.mean(yc * yc, axis=0, keepdims=True)
        yn = yc * lax.rsqrt(var + LNX_EPS)
        bonus = jnp.sum(r_ref[t] * kf * rk_ref[...], axis=0, keepdims=True)
        y_ref[t] = (yn * lg_ref[...] + lb_ref[...] + bonus * vt).astype(y_ref.dtype)
        return carry

    lax.fori_loop(0, steps, step, 0)


def _scan(r, w, k, v, lr, kk_t, ka_t, rk_t, lg_t, lb_t):
    seq, n, chains = r.shape
    lg = LANES if chains % LANES == 0 else chains
    steps = _pick_tile(seq, (48, 24, 16, 8, 4, 2, 1))
    data = pl.BlockSpec((steps, n, lg), lambda g, t: (t, 0, g))
    tile = pl.BlockSpec((n, lg), lambda g, t: (0, g))
    return pl.pallas_call(
        functools.partial(_scan_kernel, steps=steps, n=n),
        grid=(chains // lg, seq // steps),
        in_specs=[data] * 5 + [tile] * 5,
        out_specs=data,
        out_shape=jax.ShapeDtypeStruct((seq, n, chains), BF16),
        scratch_shapes=[pltpu.VMEM((n, n, lg), F32)] + [pltpu.VMEM((n, lg), F32)] * 3,
        compiler_params=_params("parallel", "arbitrary"),
        name="rwkv_scan",
    )(r, w, k, v, lr, kk_t, ka_t, rk_t, lg_t, lb_t)


def _to_chains(x, bsz, seq, heads):
    return x.reshape(bsz, seq, heads, HEAD_RWKV).transpose(1, 3, 0, 2).reshape(seq, HEAD_RWKV, bsz * heads)


def _from_chains(y, bsz, seq, heads):
    return y.reshape(seq, HEAD_RWKV, bsz, heads).transpose(2, 0, 3, 1).reshape(bsz, seq, heads * HEAD_RWKV)


def _chain_tile(p, bsz, heads):
    return jnp.tile(p.reshape(heads, HEAD_RWKV).T, (1, bsz))


def _cumsum_kernel(f_ref, b_ref, c_ref):
    z = f_ref[...] + b_ref[...]
    c = jnp.minimum(z, 0.0) - jnp.log(1.0 + jnp.exp(-jnp.abs(z)))
    seq = c.shape[0]
    rows = lax.broadcasted_iota(jnp.int32, c.shape, 0)
    shift = 1
    while shift < seq:
        c = c + jnp.where(rows < shift, 0.0, pltpu.roll(c, shift, 0))
        shift *= 2
    c_ref[...] = c


def _forget_cumsum(p_tail, b_f, bsz, seq):
    n_tail = p_tail.shape[-1]
    b = jnp.pad(b_f, (0, n_tail - b_f.shape[0])).reshape(1, n_tail)
    return pl.pallas_call(
        _cumsum_kernel,
        grid=(bsz,),
        in_specs=[pl.BlockSpec((None, seq, n_tail), lambda i: (i, 0, 0)),
                  pl.BlockSpec((1, n_tail), lambda i: (0, 0))],
        out_specs=pl.BlockSpec((None, seq, n_tail), lambda i: (i, 0, 0)),
        out_shape=jax.ShapeDtypeStruct((bsz, seq, n_tail), F32),
        compiler_params=_params("parallel"),
        name="forget_cumsum",
    )(p_tail.reshape(bsz, seq, n_tail), b)


def _fox_kernel(q_ref, k_ref, v_ref, c_ref, cr_ref, o_ref,
                qs_ref, ks_ref, vs_ref, cc_ref, *, seq, seq_pad):
    dh = q_ref.shape[-1]
    if seq_pad > seq:
        pad = jnp.zeros((seq_pad - seq, dh), BF16)
        qs_ref[pl.ds(seq, seq_pad - seq), :] = pad
        ks_ref[pl.ds(seq, seq_pad - seq), :] = pad
        vs_ref[pl.ds(seq, seq_pad - seq), :] = pad
        cc_ref[pl.ds(seq, seq_pad - seq), :] = jnp.zeros((seq_pad - seq, 1), F32)
    c_all = c_ref[...]
    lane = lax.broadcasted_iota(jnp.int32, c_all.shape, 1)
    cc_ref[pl.ds(0, seq), :] = jnp.sum(jnp.where(lane == pl.program_id(1), c_all, 0.0),
                                       axis=-1, keepdims=True)
    qs_ref[pl.ds(0, seq), :] = q_ref[...]
    ks_ref[pl.ds(0, seq), :] = k_ref[...]
    vs_ref[pl.ds(0, seq), :] = v_ref[...]

    def scores(q0, nq):
        return lax.dot_general(qs_ref[pl.ds(q0, nq), :], ks_ref[pl.ds(0, q0 + nq), :],
                               (((1,), (1,)), ((), ())), preferred_element_type=F32)

    chunks = [(q0, min(FOX_Q_ROWS, seq_pad - q0)) for q0 in range(0, seq_pad, FOX_Q_ROWS)]
    qk_next = scores(*chunks[0])
    for ci, (q0, nq) in enumerate(chunks):
        q1 = q0 + nq
        qk = qk_next
        if ci + 1 < len(chunks):
            qk_next = scores(*chunks[ci + 1])
        u = qk - cr_ref[:, pl.ds(0, q1)] * LOG2E
        rows = lax.broadcasted_iota(jnp.int32, (nq, nq), 0)
        cols = lax.broadcasted_iota(jnp.int32, (nq, nq), 1)
        u_diag = jnp.where(rows >= cols, u[:, q0:], -jnp.inf)
        mu = jnp.max(u_diag, axis=-1, keepdims=True)
        if q0 > 0:
            mu = jnp.maximum(mu, jnp.max(u[:, :q0], axis=-1, keepdims=True))
        ct = cc_ref[pl.ds(q0, nq), :] * LOG2E
        shift = ct - (mu + ct)
        e = jnp.exp2(u_diag + shift)
        if q0 > 0:
            e = jnp.concatenate([jnp.exp2(u[:, :q0] + shift), e], axis=1)
        l = jnp.sum(e, axis=-1, keepdims=True)
        ob = jnp.dot(e.astype(BF16), vs_ref[pl.ds(0, q1), :], preferred_element_type=F32) / l
        n_valid = min(q1, seq) - q0
        if n_valid > 0:
            o_ref[pl.ds(q0, n_valid), :] = ob[:n_valid].astype(BF16)


def _fox_attention(qkv, c, bsz, seq, heads):
    dh = HEAD_FOX
    seq_pad = -(-seq // Q_BLOCK) * Q_BLOCK
    pm = qkv.reshape(bsz, seq, -1)
    n_tail = c.shape[-1]
    ct = jnp.pad(c[:, :, :heads].transpose(0, 2, 1), ((0, 0), (0, 0), (0, seq_pad - seq)))
    c_row = ct[:, :, None, :]

    def seg(s):
        return pl.BlockSpec((None, seq, dh), lambda b, h, s=s: (b, 0, s * heads + h))

    return pl.pallas_call(
        functools.partial(_fox_kernel, seq=seq, seq_pad=seq_pad),
        grid=(bsz, heads),
        in_specs=[
            seg(0), seg(1), seg(2),
            pl.BlockSpec((None, seq, n_tail), lambda b, h: (b, 0, 0)),
            pl.BlockSpec((None, None, 1, seq_pad), lambda b, h: (b, h, 0, 0)),
        ],
        out_specs=pl.BlockSpec((None, seq, dh), lambda b, h: (b, 0, h)),
        out_shape=jax.ShapeDtypeStruct((bsz, seq, heads * dh), BF16),
        scratch_shapes=[pltpu.VMEM((seq_pad, dh), BF16)] * 3 + [pltpu.VMEM((seq_pad, 1), F32)],
        compiler_params=_params("parallel", "parallel"),
        name="fox_attention",
    )(pm, pm, pm, c, c_row)


def _pad_cols(w, n):
    return jnp.pad(w, ((0, 0), (0, n - w.shape[1])))


def _round_up(n, m):
    return -(-n // m) * m


def kernel(x, meta, norm_g, ffn_in, ffn_out, e_w_in, e_conv_w, e_mu, e_w0, e_w2, e_a0, e_a2,
           e_g2, e_k_k, e_k_a, e_r_k, e_lnx_g, e_lnx_b, e_w_out, o_w_in, o_b_f, o_q_g, o_k_g,
           o_w_out):
    bsz, seq_x, d = x.shape
    depth = norm_g.shape[0]
    seq = seq_x + meta.shape[0]
    d_conv = e_conv_w.shape[-1]
    d_rwkv = e_w0.shape[-1]
    h_rwkv = d_rwkv // HEAD_RWKV
    h_fox = d // HEAD_FOX
    n_even_main = 3 * d_conv + 3 * d_rwkv
    n_odd_main = 4 * d

    h = jnp.concatenate([jnp.broadcast_to(meta[None].astype(x.dtype), (bsz,) + meta.shape), x], axis=1)
    h = h.reshape(bsz * seq, d)

    ffn_in_b = ffn_in.astype(BF16)
    ffn_out_b = ffn_out.astype(BF16)
    e_w_in_b = e_w_in.astype(BF16)
    e_w_out_b = e_w_out.astype(BF16)
    o_w_in_b = o_w_in.astype(BF16)
    o_w_out_b = o_w_out.astype(BF16)

    def tail_weights(w_in, n_main):
        n_tail = _round_up(w_in.shape[1] - n_main, LANES)
        return _pad_cols(w_in[:, n_main:], n_tail).astype(BF16)

    for l in range(depth):
        g = norm_g[l]
        i = l // 2
        h = _ffn(h, g[0], ffn_in_b, ffn_out_b, g[1], l, 0)
        if l % 2 == 0:
            p_main, p_tail = _proj(h, g[2], e_w_in_b, i, n_even_main,
                                   tail_weights(e_w_in[i], n_even_main))
            y_a, r, w, k, v, lr, gate = _even_prep(
                p_main, p_tail, e_conv_w[i], e_mu[i], e_w0[i], e_w2[i], e_a0[i], e_a2[i], e_g2[i],
                bsz, seq)
            tc = functools.partial(_to_chains, bsz=bsz, seq=seq, heads=h_rwkv)
            tl = functools.partial(_chain_tile, bsz=bsz, heads=h_rwkv)
            y = _scan(tc(r), tc(w), tc(k), tc(v), tc(lr), tl(e_k_k[i]), tl(e_k_a[i]),
                      tl(e_r_k[i]), tl(e_lnx_g[i]), tl(e_lnx_b[i]))
            y_b = _from_chains(y, bsz, seq, h_rwkv)
            h = _out_even(h, y_a.reshape(bsz * seq, d_conv), y_b.reshape(bsz * seq, d_rwkv),
                          gate.reshape(bsz * seq, d_rwkv), e_w_out_b, i, g[3])
        else:
            qkv, og, p_tail = _proj_odd(h, g[2], o_w_in_b, i, tail_weights(o_w_in[i], n_odd_main),
                                        o_q_g[i], o_k_g[i])
            c = _forget_cumsum(p_tail, o_b_f[i], bsz, seq)
            o = _fox_attention(qkv, c, bsz, seq, h_fox)
            h = _out_odd(h, o.reshape(bsz * seq, d), og, o_w_out_b, i, g[3])
        h = _ffn(h, g[4], ffn_in_b, ffn_out_b, g[5], l, 1)
    return h.reshape(bsz, seq, d)[:, meta.shape[0]:]
```

```python
import functools
import math

import jax
import jax.numpy as jnp
from jax import lax
from jax.experimental import pallas as pl
from jax.experimental.pallas import tpu as pltpu

F32 = jnp.float32
BF16 = jnp.bfloat16

N_META = 16
EPS = 1e-6
LNX_EPS = 64e-5
HEAD_RWKV = 64
HEAD_FOX = 128
CONV_W = 3
Q_BLOCK = 128
FOX_Q_ROWS = 256
LOG2E = math.log2(math.e)

LANES = 128
VMEM_LIMIT = 56 * 1024 * 1024


def _params(*sem):
    return pltpu.CompilerParams(dimension_semantics=sem, vmem_limit_bytes=VMEM_LIMIT)


def _pick_tile(n, candidates):
    for c in candidates:
        if n % c == 0:
            return c
    return n


def _rms_scale(x):
    return lax.rsqrt(jnp.mean(x * x, axis=-1, keepdims=True) + EPS)


def _sigmoid(x):
    return 1.0 / (1.0 + jnp.exp(-x))


def _ffn_kernel(x_ref, gi_ref, wg_ref, wu_ref, wo_ref, go_ref, o_ref, xn_ref):
    k = pl.program_id(1)
    nk = pl.num_programs(1)

    @pl.when(k == 0)
    def _():
        x = x_ref[...]
        xn_ref[...] = (x * _rms_scale(x) * gi_ref[...]).astype(BF16)
        o_ref[...] = jnp.zeros_like(o_ref)

    xn = xn_ref[...]
    gate = jnp.dot(xn, wg_ref[...], preferred_element_type=F32)
    up = jnp.dot(xn, wu_ref[...], preferred_element_type=F32)
    mid = (gate * _sigmoid(gate) * up).astype(BF16)
    o_ref[...] += jnp.dot(mid, wo_ref[...], preferred_element_type=F32)

    @pl.when(k == nk - 1)
    def _():
        y = o_ref[...]
        o_ref[...] = x_ref[...] + 0.5 * (y * _rms_scale(y) * go_ref[...])


def _ffn(h, g_in, w_in, w_out, g_out, layer, slot):
    t, d = h.shape
    d_ff = w_out.shape[2]
    tm = _pick_tile(t, (768, 512, 384, 256, 128, 64, 32, 16))
    tf = _pick_tile(d_ff, (512, 256, 128))
    nk = d_ff // tf
    return pl.pallas_call(
        _ffn_kernel,
        grid=(t // tm, nk),
        in_specs=[
            pl.BlockSpec((tm, d), lambda i, k: (i, 0)),
            pl.BlockSpec((1, d), lambda i, k: (0, 0)),
            pl.BlockSpec((None, None, d, tf), lambda i, k: (layer, slot, 0, k)),
            pl.BlockSpec((None, None, d, tf), lambda i, k: (layer, slot, 0, nk + k)),
            pl.BlockSpec((None, None, tf, d), lambda i, k: (layer, slot, k, 0)),
            pl.BlockSpec((1, d), lambda i, k: (0, 0)),
        ],
        out_specs=pl.BlockSpec((tm, d), lambda i, k: (i, 0)),
        out_shape=jax.ShapeDtypeStruct((t, d), F32),
        scratch_shapes=[pltpu.VMEM((tm, d), BF16)],
        compiler_params=_params("parallel", "arbitrary"),
        name="ffn",
    )(h, g_in.reshape(1, d), w_in, w_in, w_out, g_out.reshape(1, d))


def _proj_even_kernel(x_ref, g_ref, wm_ref, wt_ref, lo_ref, hi_ref, pt_ref, xn_ref, *, lo_blocks):
    j = pl.program_id(1)

    @pl.when(j == 0)
    def _():
        x = x_ref[...]
        xn = (x * _rms_scale(x) * g_ref[...]).astype(BF16)
        xn_ref[...] = xn
        pt_ref[...] = jnp.dot(xn, wt_ref[...], preferred_element_type=F32)

    def project():
        return jnp.dot(xn_ref[...], wm_ref[...], preferred_element_type=F32)

    @pl.when(j < lo_blocks)
    def _():
        lo_ref[...] = project().astype(BF16)

    @pl.when(j >= lo_blocks)
    def _():
        hi_ref[...] = project()


def _proj_even(h, g, w_all, idx, n_lo, n_hi, w_tail):
    t, d = h.shape
    n_tail = w_tail.shape[1]
    tm = _pick_tile(t, (768, 512, 384, 256, 128, 64, 32, 16))
    tn = _pick_tile(math.gcd(n_lo, n_hi), (1024, 512, 256, 128))
    lo_blocks = n_lo // tn
    return pl.pallas_call(
        functools.partial(_proj_even_kernel, lo_blocks=lo_blocks),
        grid=(t // tm, (n_lo + n_hi) // tn),
        in_specs=[
            pl.BlockSpec((tm, d), lambda i, j: (i, 0)),
            pl.BlockSpec((1, d), lambda i, j: (0, 0)),
            pl.BlockSpec((None, d, tn), lambda i, j: (idx, 0, j)),
            pl.BlockSpec((d, n_tail), lambda i, j: (0, 0)),
        ],
        out_specs=[
            pl.BlockSpec((tm, tn), lambda i, j: (i, jnp.minimum(j, lo_blocks - 1))),
            pl.BlockSpec((tm, tn), lambda i, j: (i, jnp.maximum(j - lo_blocks, 0))),
            pl.BlockSpec((tm, n_tail), lambda i, j: (i, 0)),
        ],
        out_shape=[
            jax.ShapeDtypeStruct((t, n_lo), BF16),
            jax.ShapeDtypeStruct((t, n_hi), F32),
            jax.ShapeDtypeStruct((t, n_tail), F32),
        ],
        scratch_shapes=[pltpu.VMEM((tm, d), BF16)],
        compiler_params=_params("parallel", "arbitrary"),
        name="proj_even",
    )(h, g.reshape(1, d), w_all, w_tail)


def _proj_odd_kernel(x_ref, g_ref, wm_ref, wt_ref, qg_ref, kg_ref, qkv_ref, og_ref, pt_ref,
                     xn_ref, *, seg_blocks):
    j = pl.program_id(1)
    dh = qg_ref.shape[-1]

    @pl.when(j == 0)
    def _():
        x = x_ref[...]
        xn = (x * _rms_scale(x) * g_ref[...]).astype(BF16)
        xn_ref[...] = xn
        pt_ref[...] = jnp.dot(xn, wt_ref[...], preferred_element_type=F32)

    def project():
        return jnp.dot(xn_ref[...], wm_ref[...], preferred_element_type=F32)

    def head_norm(acc, gain, scale):
        heads = [acc[:, c:c + dh] for c in range(0, acc.shape[1], dh)]
        return jnp.concatenate([(xh * _rms_scale(xh) * gain * scale).astype(BF16) for xh in heads],
                               axis=1)

    @pl.when(j < seg_blocks)
    def _():
        qkv_ref[...] = head_norm(project(), qg_ref[...], dh ** -0.5 * LOG2E)

    @pl.when((j >= seg_blocks) & (j < 2 * seg_blocks))
    def _():
        qkv_ref[...] = head_norm(project(), kg_ref[...], 1.0)

    @pl.when((j >= 2 * seg_blocks) & (j < 3 * seg_blocks))
    def _():
        qkv_ref[...] = project().astype(BF16)

    @pl.when(j >= 3 * seg_blocks)
    def _():
        og_ref[...] = project()


def _proj_odd(h, g, w_all, idx, w_tail, q_g, k_g):
    t, d = h.shape
    n_tail = w_tail.shape[1]
    dh = q_g.shape[0]
    tm = _pick_tile(t, (768, 512, 384, 256, 128, 64, 32, 16))
    tn = _pick_tile(d, (1024, 512, 256, 128))
    sb = d // tn
    return pl.pallas_call(
        functools.partial(_proj_odd_kernel, seg_blocks=sb),
        grid=(t // tm, 4 * sb),
        in_specs=[
            pl.BlockSpec((tm, d), lambda i, j: (i, 0)),
            pl.BlockSpec((1, d), lambda i, j: (0, 0)),
            pl.BlockSpec((None, d, tn), lambda i, j: (idx, 0, j)),
            pl.BlockSpec((d, n_tail), lambda i, j: (0, 0)),
            pl.BlockSpec((1, dh), lambda i, j: (0, 0)),
            pl.BlockSpec((1, dh), lambda i, j: (0, 0)),
        ],
        out_specs=[
            pl.BlockSpec((tm, tn), lambda i, j: (i, jnp.minimum(j, 3 * sb - 1))),
            pl.BlockSpec((tm, tn), lambda i, j: (i, jnp.maximum(j - 3 * sb, 0))),
            pl.BlockSpec((tm, n_tail), lambda i, j: (i, 0)),
        ],
        out_shape=[
            jax.ShapeDtypeStruct((t, 3 * d), BF16),
            jax.ShapeDtypeStruct((t, d), F32),
            jax.ShapeDtypeStruct((t, n_tail), F32),
        ],
        scratch_shapes=[pltpu.VMEM((tm, d), BF16)],
        compiler_params=_params("parallel", "arbitrary"),
        name="proj_odd",
    )(h, g.reshape(1, d), w_all, w_tail, q_g.reshape(1, dh), k_g.reshape(1, dh))


def _out_even_kernel(h_ref, ya_ref, yb_ref, gt_ref, wa_ref, wb_ref, g_ref, o_ref):
    yb = (yb_ref[...].astype(F32) * gt_ref[...]).astype(BF16)
    m = jnp.dot(ya_ref[...], wa_ref[...], preferred_element_type=F32)
    m = m + jnp.dot(yb, wb_ref[...], preferred_element_type=F32)
    o_ref[...] = h_ref[...] + m * _rms_scale(m) * g_ref[...]


def _out_even(h, y_a, y_b, gate, w_out, idx, g, bsz, seq):
    t, d = h.shape
    da = y_a.shape[1]
    db = gate.shape[1]
    tm = _pick_tile(seq, (688, 512, 384, 256, 128, 64, 32, 16))
    nt = seq // tm
    rows = lambda width: pl.BlockSpec((tm, width), lambda b, j: (b * nt + j, 0))
    once = pl.Buffered(1)
    return pl.pallas_call(
        _out_even_kernel,
        grid=(bsz, nt),
        in_specs=[
            rows(d), rows(da),
            pl.BlockSpec((tm, db), lambda b, j: (j, b)),
            rows(db),
            pl.BlockSpec((None, da, d), lambda b, j: (idx, 0, 0), pipeline_mode=once),
            pl.BlockSpec((None, db, d), lambda b, j: (idx, da // db, 0), pipeline_mode=once),
            pl.BlockSpec((1, d), lambda b, j: (0, 0)),
        ],
        out_specs=rows(d),
        out_shape=jax.ShapeDtypeStruct((t, d), F32),
        compiler_params=_params("parallel", "parallel"),
        name="out_even",
    )(h, y_a, y_b, gate, w_out, w_out, g.reshape(1, d))


def _out_odd_kernel(h_ref, o_in_ref, og_ref, w_ref, g_ref, o_ref):
    act = (o_in_ref[...].astype(F32) * _sigmoid(og_ref[...])).astype(BF16)
    m = jnp.dot(act, w_ref[...], preferred_element_type=F32)
    o_ref[...] = h_ref[...] + m * _rms_scale(m) * g_ref[...]


def _out_odd(h, o, og, w_out, idx, g):
    t, d = h.shape
    tm = _pick_tile(t, (384, 256, 128, 64, 32, 16))
    return pl.pallas_call(
        _out_odd_kernel,
        grid=(t // tm,),
        in_specs=[
            pl.BlockSpec((tm, d), lambda i: (i, 0)),
            pl.BlockSpec((tm, d), lambda i: (i, 0)),
            pl.BlockSpec((tm, d), lambda i: (i, 0)),
            pl.BlockSpec((None, d, d), lambda i: (idx, 0, 0)),
            pl.BlockSpec((1, d), lambda i: (0, 0)),
        ],
        out_specs=pl.BlockSpec((tm, d), lambda i: (i, 0)),
        out_shape=jax.ShapeDtypeStruct((t, d), F32),
        compiler_params=_params("parallel"),
        name="out_odd",
    )(h, o, og, w_out, g.reshape(1, d))


def _shift_rows(x, n):
    rows = lax.broadcasted_iota(jnp.int32, x.shape, 0)
    return jnp.where(rows < n, 0.0, pltpu.roll(x, n, 0))


def _even_prep_kernel(gb_ref, gc_ref, hv_ref, pr_ref, pk_ref, pv_ref, pl_ref,
                      cw_ref, mur_ref, muk_ref, muv_ref, mul_ref,
                      w0_ref, w2_ref, a0_ref, a2_ref, g2_ref,
                      ya_ref, r_ref, w_ref, k_ref, v_ref, a_ref, g_ref):
    z = gc_ref[...].astype(F32) * hv_ref[...].astype(F32)
    cw = cw_ref[...]
    conv = cw[0:1] * _shift_rows(z, 2) + cw[1:2] * _shift_rows(z, 1) + cw[2:3] * z
    ya_ref[...] = (gb_ref[...].astype(F32) * conv).astype(BF16)

    def lerp(ref, mu_ref):
        cur = ref[...]
        return cur + (_shift_rows(cur, 1) - cur) * mu_ref[...]

    r = lerp(pr_ref, mur_ref)
    k = lerp(pk_ref, muk_ref)
    v = lerp(pv_ref, muv_ref)
    ul = lerp(pl_ref, mul_ref)
    xw = w0_ref[...] + jnp.dot(jnp.tanh(ul).astype(BF16), w2_ref[...], preferred_element_type=F32)
    w_ref[...] = jnp.exp(-math.exp(-0.5) * _sigmoid(xw))
    a_ref[...] = _sigmoid(a0_ref[...] + jnp.dot(ul.astype(BF16), a2_ref[...],
                                                preferred_element_type=F32))
    g_ref[...] = jnp.dot(_sigmoid(ul).astype(BF16), g2_ref[...], preferred_element_type=F32)
    r_ref[...] = r
    k_ref[...] = k
    v_ref[...] = v


def _even_prep(p_conv, p_rwkv, p_tail, conv_w, mu, w0, w2, a0, a2, g2, bsz, seq):
    d_conv = conv_w.shape[1]
    d_rwkv = w0.shape[0]
    decay_rank, aaa_rank, gate_rank = w2.shape[0], a2.shape[0], g2.shape[0]
    n_tail = p_tail.shape[-1]
    tc = LANES
    nc = d_rwkv // tc
    assert d_conv == d_rwkv and d_conv % tc == 0
    pc = p_conv.reshape(bsz, seq, -1)
    pm = p_rwkv.reshape(bsz, seq, -1)
    pt = p_tail.reshape(bsz, seq, n_tail)
    cb = d_conv // tc

    def seg(s):
        return pl.BlockSpec((None, seq, tc), lambda b, c, s=s: (b, 0, s * cb + c))

    def vec(n=tc):
        return pl.BlockSpec((1, n), lambda b, c: (0, c))

    def full(shape):
        return pl.BlockSpec(shape, lambda b, c: (0,) * len(shape))

    mu_main = mu[:3 * d_rwkv].reshape(1, 3 * d_rwkv)
    mu_tail = jnp.pad(mu[3 * d_rwkv:], (0, n_tail - (mu.shape[0] - 3 * d_rwkv))).reshape(1, n_tail)

    def mu_seg(s):
        return pl.BlockSpec((1, tc), lambda b, c, s=s: (0, s * cb + c))

    def tail_rows(w, start):
        return jnp.pad(w, ((start, n_tail - start - w.shape[0]), (0, 0))).astype(BF16)

    w2p = tail_rows(w2, 0)
    a2p = tail_rows(a2, decay_rank)
    g2p = tail_rows(g2, decay_rank + aaa_rank)
    lowrank = pl.BlockSpec((n_tail, tc), lambda b, c: (0, c))

    out_spec = pl.BlockSpec((None, seq, tc), lambda b, c: (b, 0, c))
    tm_spec = pl.BlockSpec((seq, tc), lambda b, c: (0, b * nc + c))
    tm_shape = jax.ShapeDtypeStruct((seq, bsz * d_rwkv), F32)
    return pl.pallas_call(
        _even_prep_kernel,
        grid=(bsz, nc),
        in_specs=[
            seg(0), seg(1), seg(2), seg(0), seg(1), seg(2),
            pl.BlockSpec((None, seq, n_tail), lambda b, c: (b, 0, 0)),
            pl.BlockSpec((CONV_W, tc), lambda b, c: (0, c)),
            mu_seg(0), mu_seg(1), mu_seg(2), full((1, n_tail)),
            vec(), lowrank, vec(), lowrank, lowrank,
        ],
        out_specs=[out_spec] + [tm_spec] * 5 + [out_spec],
        out_shape=[jax.ShapeDtypeStruct((bsz, seq, d_conv), BF16)] + [tm_shape] * 5
        + [jax.ShapeDtypeStruct((bsz, seq, d_rwkv), F32)],
        compiler_params=_params("parallel", "arbitrary"),
        name="even_prep",
    )(pc, pc, pc, pm, pm, pm, pt, conv_w, mu_main, mu_main, mu_main, mu_tail,
      w0.reshape(1, -1), w2p, a0.reshape(1, -1), a2p, g2p)


def _scan_kernel(r_ref, w_ref, k_ref, v_ref, lr_ref, kkw_ref, kaw_ref, rk_ref, lg_ref, lb_ref,
                 y_ref, s_ref, kk_ref, kb_ref, kf_ref, *, steps, n):
    @pl.when(pl.program_id(1) == 0)
    def _():
        s_ref[...] = jnp.zeros_like(s_ref)

    def step(t, carry):
        kt = k_ref[t]
        lr = lr_ref[t]
        kr = kt * kkw_ref[...]
        nrm = jnp.sqrt(jnp.sum(kr * kr, axis=0, keepdims=True))
        kk = kr / jnp.maximum(nrm, 1e-12)
        kf = kt * (1.0 + (lr - 1.0) * kaw_ref[...])
        kk_ref[...] = kk
        kb_ref[...] = kk * lr
        kf_ref[...] = kf
        vt = v_ref[t]

        sa = jnp.zeros_like(vt)
        for k in range(n):
            sa = sa + s_ref[k] * kk_ref[pl.ds(k, 1), :]
        sa = -sa

        y = jnp.zeros_like(vt)
        for k in range(n):
            sk = (s_ref[k] * w_ref[t, pl.ds(k, 1), :] + sa * kb_ref[pl.ds(k, 1), :]
                  + vt * kf_ref[pl.ds(k, 1), :])
            s_ref[k] = sk
            y = y + sk * r_ref[t, pl.ds(k, 1), :]

        mean = jnp.mean(y, axis=0, keepdims=True)
        yc = y - mean
        var = jnp.mean(yc * yc, axis=0, keepdims=True)
        yn = yc * lax.rsqrt(var + LNX_EPS)
        bonus = jnp.sum(r_ref[t] * kf * rk_ref[...], axis=0, keepdims=True)
        y_ref[t] = (yn * lg_ref[...] + lb_ref[...] + bonus * vt).astype(y_ref.dtype)
        return carry

    lax.fori_loop(0, steps, step, 0)


def _scan(r, w, k, v, lr, kk_t, ka_t, rk_t, lg_t, lb_t):
    seq, n, chains = r.shape
    lg = LANES if chains % LANES == 0 else chains
    steps = _pick_tile(seq, (48, 24, 16, 8, 4, 2, 1))
    data = pl.BlockSpec((steps, n, lg), lambda g, t: (t, 0, g))
    tile = pl.BlockSpec((n, lg), lambda g, t: (0, g))
    return pl.pallas_call(
        functools.partial(_scan_kernel, steps=steps, n=n),
        grid=(chains // lg, seq // steps),
        in_specs=[data] * 5 + [tile] * 5,
        out_specs=data,
        out_shape=jax.ShapeDtypeStruct((seq, n, chains), BF16),
        scratch_shapes=[pltpu.VMEM((n, n, lg), F32)] + [pltpu.VMEM((n, lg), F32)] * 3,
        compiler_params=_params("parallel", "arbitrary"),
        name="rwkv_scan",
    )(r, w, k, v, lr, kk_t, ka_t, rk_t, lg_t, lb_t)


def _to_chains(x, chains):
    return jnp.swapaxes(x.reshape(x.shape[0], chains, HEAD_RWKV), 1, 2)


def _from_chains(y):
    return jnp.swapaxes(y, 1, 2).reshape(y.shape[0], -1)


def _chain_tile(p, bsz, heads):
    return jnp.tile(p.reshape(heads, HEAD_RWKV).T, (1, bsz))


def _cumsum_kernel(f_ref, b_ref, c_ref):
    z = f_ref[...] + b_ref[...]
    c = jnp.minimum(z, 0.0) - jnp.log(1.0 + jnp.exp(-jnp.abs(z)))
    seq = c.shape[0]
    rows = lax.broadcasted_iota(jnp.int32, c.shape, 0)
    shift = 1
    while shift < seq:
        c = c + jnp.where(rows < shift, 0.0, pltpu.roll(c, shift, 0))
        shift *= 2
    c_ref[...] = c


def _forget_cumsum(p_tail, b_f, bsz, seq):
    n_tail = p_tail.shape[-1]
    b = jnp.pad(b_f, (0, n_tail - b_f.shape[0])).reshape(1, n_tail)
    return pl.pallas_call(
        _cumsum_kernel,
        grid=(bsz,),
        in_specs=[pl.BlockSpec((None, seq, n_tail), lambda i: (i, 0, 0)),
                  pl.BlockSpec((1, n_tail), lambda i: (0, 0))],
        out_specs=pl.BlockSpec((None, seq, n_tail), lambda i: (i, 0, 0)),
        out_shape=jax.ShapeDtypeStruct((bsz, seq, n_tail), F32),
        compiler_params=_params("parallel"),
        name="forget_cumsum",
    )(p_tail.reshape(bsz, seq, n_tail), b)


def _fox_kernel(q_ref, k_ref, v_ref, c_ref, cr_ref, o_ref,
                qs_ref, ks_ref, vs_ref, cc_ref, *, seq, seq_pad):
    dh = q_ref.shape[-1]
    if seq_pad > seq:
        pad = jnp.zeros((seq_pad - seq, dh), BF16)
        qs_ref[pl.ds(seq, seq_pad - seq), :] = pad
        ks_ref[pl.ds(seq, seq_pad - seq), :] = pad
        vs_ref[pl.ds(seq, seq_pad - seq), :] = pad
        cc_ref[pl.ds(seq, seq_pad - seq), :] = jnp.zeros((seq_pad - seq, 1), F32)
    c_all = c_ref[...]
    lane = lax.broadcasted_iota(jnp.int32, c_all.shape, 1)
    cc_ref[pl.ds(0, seq), :] = jnp.sum(jnp.where(lane == pl.program_id(1), c_all, 0.0),
                                       axis=-1, keepdims=True)
    qs_ref[pl.ds(0, seq), :] = q_ref[...]
    ks_ref[pl.ds(0, seq), :] = k_ref[...]
    vs_ref[pl.ds(0, seq), :] = v_ref[...]

    def scores(q0, nq):
        return lax.dot_general(qs_ref[pl.ds(q0, nq), :], ks_ref[pl.ds(0, q0 + nq), :],
                               (((1,), (1,)), ((), ())), preferred_element_type=F32)

    chunks = [(q0, min(FOX_Q_ROWS, seq_pad - q0)) for q0 in range(0, seq_pad, FOX_Q_ROWS)]
    qk_next = scores(*chunks[0])
    for ci, (q0, nq) in enumerate(chunks):
        q1 = q0 + nq
        qk = qk_next
        if ci + 1 < len(chunks):
            qk_next = scores(*chunks[ci + 1])
        u = qk - cr_ref[:, pl.ds(0, q1)] * LOG2E
        rows = lax.broadcasted_iota(jnp.int32, (nq, nq), 0)
        cols = lax.broadcasted_iota(jnp.int32, (nq, nq), 1)
        u_diag = jnp.where(rows >= cols, u[:, q0:], -jnp.inf)
        mu = jnp.max(u_diag, axis=-1, keepdims=True)
        if q0 > 0:
            mu = jnp.maximum(mu, jnp.max(u[:, :q0], axis=-1, keepdims=True))
        ct = cc_ref[pl.ds(q0, nq), :] * LOG2E
        shift = ct - (mu + ct)
        e = jnp.exp2(u_diag + shift)
        if q0 > 0:
            e = jnp.concatenate([jnp.exp2(u[:, :q0] + shift), e], axis=1)
        l = jnp.sum(e, axis=-1, keepdims=True)
        ob = jnp.dot(e.astype(BF16), vs_ref[pl.ds(0, q1), :], preferred_element_type=F32) / l
        n_valid = min(q1, seq) - q0
        if n_valid > 0:
            o_ref[pl.ds(q0, n_valid), :] = ob[:n_valid].astype(BF16)


def _fox_attention(qkv, c, bsz, seq, heads):
    dh = HEAD_FOX
    seq_pad = -(-seq // Q_BLOCK) * Q_BLOCK
    pm = qkv.reshape(bsz, seq, -1)
    n_tail = c.shape[-1]
    ct = jnp.pad(c[:, :, :heads].transpose(0, 2, 1), ((0, 0), (0, 0), (0, seq_pad - seq)))
    c_row = ct[:, :, None, :]

    def seg(s):
        return pl.BlockSpec((None, seq, dh), lambda b, h, s=s: (b, 0, s * heads + h))

    return pl.pallas_call(
        functools.partial(_fox_kernel, seq=seq, seq_pad=seq_pad),
        grid=(bsz, heads),
        in_specs=[
            seg(0), seg(1), seg(2),
            pl.BlockSpec((None, seq, n_tail), lambda b, h: (b, 0, 0)),
            pl.BlockSpec((None, None, 1, seq_pad), lambda b, h: (b, h, 0, 0)),
        ],
        out_specs=pl.BlockSpec((None, seq, dh), lambda b, h: (b, 0, h)),
        out_shape=jax.ShapeDtypeStruct((bsz, seq, heads * dh), BF16),
        scratch_shapes=[pltpu.VMEM((seq_pad, dh), BF16)] * 3 + [pltpu.VMEM((seq_pad, 1), F32)],
        compiler_params=_params("parallel", "parallel"),
        name="fox_attention",
    )(pm, pm, pm, c, c_row)


def _pad_cols(w, n):
    return jnp.pad(w, ((0, 0), (0, n - w.shape[1])))


def _round_up(n, m):
    return -(-n // m) * m


def kernel(x, meta, norm_g, ffn_in, ffn_out, e_w_in, e_conv_w, e_mu, e_w0, e_w2, e_a0, e_a2,
           e_g2, e_k_k, e_k_a, e_r_k, e_lnx_g, e_lnx_b, e_w_out, o_w_in, o_b_f, o_q_g, o_k_g,
           o_w_out):
    bsz, seq_x, d = x.shape
    depth = norm_g.shape[0]
    seq = seq_x + meta.shape[0]
    d_conv = e_conv_w.shape[-1]
    d_rwkv = e_w0.shape[-1]
    h_rwkv = d_rwkv // HEAD_RWKV
    h_fox = d // HEAD_FOX
    n_even_main = 3 * d_conv + 3 * d_rwkv
    n_odd_main = 4 * d

    h = jnp.concatenate([jnp.broadcast_to(meta[None].astype(x.dtype), (bsz,) + meta.shape), x], axis=1)
    h = h.reshape(bsz * seq, d)

    ffn_in_b = ffn_in.astype(BF16)
    ffn_out_b = ffn_out.astype(BF16)
    e_w_in_b = e_w_in.astype(BF16)
    e_w_out_b = e_w_out.astype(BF16)
    o_w_in_b = o_w_in.astype(BF16)
    o_w_out_b = o_w_out.astype(BF16)

    def tail_weights(w_in, n_main):
        n_tail = _round_up(w_in.shape[1] - n_main, LANES)
        return _pad_cols(w_in[:, n_main:], n_tail).astype(BF16)

    for l in range(depth):
        g = norm_g[l]
        i = l // 2
        h = _ffn(h, g[0], ffn_in_b, ffn_out_b, g[1], l, 0)
        if l % 2 == 0:
            p_conv, p_rwkv, p_tail = _proj_even(h, g[2], e_w_in_b, i, 3 * d_conv, 3 * d_rwkv,
                                                tail_weights(e_w_in[i], n_even_main))
            y_a, r, w, k, v, lr, gate = _even_prep(
                p_conv, p_rwkv, p_tail, e_conv_w[i], e_mu[i], e_w0[i], e_w2[i], e_a0[i], e_a2[i],
                e_g2[i], bsz, seq)
            tc = functools.partial(_to_chains, chains=bsz * h_rwkv)
            tl = functools.partial(_chain_tile, bsz=bsz, heads=h_rwkv)
            y = _scan(tc(r), tc(w), tc(k), tc(v), tc(lr), tl(e_k_k[i]), tl(e_k_a[i]),
                      tl(e_r_k[i]), tl(e_lnx_g[i]), tl(e_lnx_b[i]))
            h = _out_even(h, y_a.reshape(bsz * seq, d_conv), _from_chains(y),
                          gate.reshape(bsz * seq, d_rwkv), e_w_out_b, i, g[3], bsz, seq)
        else:
            qkv, og, p_tail = _proj_odd(h, g[2], o_w_in_b, i, tail_weights(o_w_in[i], n_odd_main),
                                        o_q_g[i], o_k_g[i])
            c = _forget_cumsum(p_tail, o_b_f[i], bsz, seq)
            o = _fox_attention(qkv, c, bsz, seq, h_fox)
            h = _out_odd(h, o.reshape(bsz * seq, d), og, o_w_out_b, i, g[3])
        h = _ffn(h, g[4], ffn_in_b, ffn_out_b, g[5], l, 1)
    return h.reshape(bsz, seq, d)[:, meta.shape[0]:]
```

```python
import functools
import math

import jax
import jax.numpy as jnp
from jax import lax
from jax.experimental import pallas as pl
from jax.experimental.pallas import tpu as pltpu

F32 = jnp.float32
BF16 = jnp.bfloat16

N_META = 16
EPS = 1e-6
LNX_EPS = 64e-5
HEAD_RWKV = 64
HEAD_FOX = 128
CONV_W = 3
Q_BLOCK = 128
FOX_Q_ROWS = 256
LOG2E = math.log2(math.e)

LANES = 128
VMEM_LIMIT = 56 * 1024 * 1024


def _params(*sem):
    return pltpu.CompilerParams(dimension_semantics=sem, vmem_limit_bytes=VMEM_LIMIT)


def _pick_tile(n, candidates):
    for c in candidates:
        if n % c == 0:
            return c
    return n


def _rms_scale(x):
    return lax.rsqrt(jnp.mean(x * x, axis=-1, keepdims=True) + EPS)


def _sigmoid(x):
    return 1.0 / (1.0 + jnp.exp(-x))


def _ffn_kernel(x_ref, gi_ref, wg_ref, wu_ref, wo_ref, go_ref, o_ref, xn_ref):
    k = pl.program_id(1)
    nk = pl.num_programs(1)

    @pl.when(k == 0)
    def _():
        x = x_ref[...]
        xn_ref[...] = (x * _rms_scale(x) * gi_ref[...]).astype(BF16)
        o_ref[...] = jnp.zeros_like(o_ref)

    xn = xn_ref[...]
    gate = jnp.dot(xn, wg_ref[...], preferred_element_type=F32)
    up = jnp.dot(xn, wu_ref[...], preferred_element_type=F32)
    mid = (gate * _sigmoid(gate) * up).astype(BF16)
    o_ref[...] += jnp.dot(mid, wo_ref[...], preferred_element_type=F32)

    @pl.when(k == nk - 1)
    def _():
        y = o_ref[...]
        o_ref[...] = x_ref[...] + 0.5 * (y * _rms_scale(y) * go_ref[...])


def _ffn(h, g_in, w_in, w_out, g_out, layer, slot):
    t, d = h.shape
    d_ff = w_out.shape[2]
    tm = _pick_tile(t, (768, 512, 384, 256, 128, 64, 32, 16))
    tf = _pick_tile(d_ff, (512, 256, 128))
    nk = d_ff // tf
    return pl.pallas_call(
        _ffn_kernel,
        grid=(t // tm, nk),
        in_specs=[
            pl.BlockSpec((tm, d), lambda i, k: (i, 0)),
            pl.BlockSpec((1, d), lambda i, k: (0, 0)),
            pl.BlockSpec((None, None, d, tf), lambda i, k: (layer, slot, 0, k)),
            pl.BlockSpec((None, None, d, tf), lambda i, k: (layer, slot, 0, nk + k)),
            pl.BlockSpec((None, None, tf, d), lambda i, k: (layer, slot, k, 0)),
            pl.BlockSpec((1, d), lambda i, k: (0, 0)),
        ],
        out_specs=pl.BlockSpec((tm, d), lambda i, k: (i, 0)),
        out_shape=jax.ShapeDtypeStruct((t, d), F32),
        scratch_shapes=[pltpu.VMEM((tm, d), BF16)],
        compiler_params=_params("parallel", "arbitrary"),
        name="ffn",
    )(h, g_in.reshape(1, d), w_in, w_in, w_out, g_out.reshape(1, d))


N_CONV_SEG = 3


def _proj_even_kernel(x_ref, g_ref, wm_ref, wt_ref, cw_ref, mu_ref, ya_ref, rkv_ref, pt_ref,
                      xn_ref, gb_ref, gc_ref, zc_ref, pc_ref, *, tiles_per_seq):
    i = pl.program_id(0)
    j = pl.program_id(1)
    tm = x_ref.shape[0]

    @pl.when(j == 0)
    def _():
        x = x_ref[...]
        xn = (x * _rms_scale(x) * g_ref[...]).astype(BF16)
        xn_ref[...] = xn
        pt_ref[...] = jnp.dot(xn, wt_ref[...], preferred_element_type=F32)

        @pl.when(i % tiles_per_seq == 0)
        def _():
            zc_ref[...] = jnp.zeros_like(zc_ref)
            pc_ref[...] = jnp.zeros_like(pc_ref)

    def project():
        return jnp.dot(xn_ref[...], wm_ref[...], preferred_element_type=F32)

    def shifted(cur, carry, n):
        rows = lax.broadcasted_iota(jnp.int32, cur.shape, 0)
        out = pltpu.roll(cur, n, 0)
        for r in range(n):
            out = jnp.where(rows == r, carry[r:r + 1], out)
        return out

    @pl.when(j == 0)
    def _():
        gb_ref[...] = project()

    @pl.when(j == 1)
    def _():
        gc_ref[...] = project()

    @pl.when(j == 2)
    def _():
        z = gc_ref[...] * project()
        zc = zc_ref[...]
        cw = cw_ref[...]
        conv = (cw[0:1] * shifted(z, zc[0:2], 2) + cw[1:2] * shifted(z, zc[1:2], 1)
                + cw[2:3] * z)
        ya_ref[...] = (gb_ref[...] * conv).astype(BF16)
        zc_ref[0:2] = z[tm - 2:tm]

    @pl.when(j >= N_CONV_SEG)
    def _():
        s = j - N_CONV_SEG
        p = project()
        rkv_ref[...] = p + (shifted(p, pc_ref[s, 0:1], 1) - p) * mu_ref[...]
        pc_ref[s, 0:1] = p[tm - 1:tm]


def _proj_even(h, g, w_all, idx, w_tail, conv_w, mu_rkv, bsz, seq):
    t, d = h.shape
    n_tail = w_tail.shape[1]
    seg = conv_w.shape[1]
    tm = _pick_tile(seq, (688, 512, 384, 256, 128, 64, 32, 16))
    nt = seq // tm
    rkv_seg = lambda j: jnp.maximum(j - N_CONV_SEG, 0)
    return pl.pallas_call(
        functools.partial(_proj_even_kernel, tiles_per_seq=nt),
        grid=(bsz * nt, 2 * N_CONV_SEG),
        in_specs=[
            pl.BlockSpec((tm, d), lambda i, j: (i, 0)),
            pl.BlockSpec((1, d), lambda i, j: (0, 0)),
            pl.BlockSpec((None, d, seg), lambda i, j: (idx, 0, j)),
            pl.BlockSpec((d, n_tail), lambda i, j: (0, 0)),
            pl.BlockSpec((CONV_W, seg), lambda i, j: (0, 0)),
            pl.BlockSpec((None, 1, seg), lambda i, j: (rkv_seg(j), 0, 0)),
        ],
        out_specs=[
            pl.BlockSpec((tm, seg), lambda i, j: (i, 0)),
            pl.BlockSpec((None, tm, seg), lambda i, j: (rkv_seg(j), i % nt, i // nt)),
            pl.BlockSpec((tm, n_tail), lambda i, j: (i, 0)),
        ],
        out_shape=[
            jax.ShapeDtypeStruct((t, seg), BF16),
            jax.ShapeDtypeStruct((N_CONV_SEG, seq, bsz * seg), F32),
            jax.ShapeDtypeStruct((t, n_tail), F32),
        ],
        scratch_shapes=[pltpu.VMEM((tm, d), BF16), pltpu.VMEM((tm, seg), F32),
                        pltpu.VMEM((tm, seg), F32), pltpu.VMEM((8, seg), F32),
                        pltpu.VMEM((N_CONV_SEG, 8, seg), F32)],
        compiler_params=_params("arbitrary", "arbitrary"),
        name="proj_even",
    )(h, g.reshape(1, d), w_all, w_tail, conv_w, mu_rkv)


def _proj_odd_kernel(x_ref, g_ref, wm_ref, wt_ref, qg_ref, kg_ref, qkv_ref, og_ref, pt_ref,
                     xn_ref, *, seg_blocks):
    j = pl.program_id(1)
    dh = qg_ref.shape[-1]

    @pl.when(j == 0)
    def _():
        x = x_ref[...]
        xn = (x * _rms_scale(x) * g_ref[...]).astype(BF16)
        xn_ref[...] = xn
        pt_ref[...] = jnp.dot(xn, wt_ref[...], preferred_element_type=F32)

    def project():
        return jnp.dot(xn_ref[...], wm_ref[...], preferred_element_type=F32)

    def head_norm(acc, gain, scale):
        heads = [acc[:, c:c + dh] for c in range(0, acc.shape[1], dh)]
        return jnp.concatenate([(xh * _rms_scale(xh) * gain * scale).astype(BF16) for xh in heads],
                               axis=1)

    @pl.when(j < seg_blocks)
    def _():
        qkv_ref[...] = head_norm(project(), qg_ref[...], dh ** -0.5 * LOG2E)

    @pl.when((j >= seg_blocks) & (j < 2 * seg_blocks))
    def _():
        qkv_ref[...] = head_norm(project(), kg_ref[...], 1.0)

    @pl.when((j >= 2 * seg_blocks) & (j < 3 * seg_blocks))
    def _():
        qkv_ref[...] = project().astype(BF16)

    @pl.when(j >= 3 * seg_blocks)
    def _():
        og_ref[...] = project()


def _proj_odd(h, g, w_all, idx, w_tail, q_g, k_g):
    t, d = h.shape
    n_tail = w_tail.shape[1]
    dh = q_g.shape[0]
    tm = _pick_tile(t, (768, 512, 384, 256, 128, 64, 32, 16))
    tn = _pick_tile(d, (1024, 512, 256, 128))
    sb = d // tn
    return pl.pallas_call(
        functools.partial(_proj_odd_kernel, seg_blocks=sb),
        grid=(t // tm, 4 * sb),
        in_specs=[
            pl.BlockSpec((tm, d), lambda i, j: (i, 0)),
            pl.BlockSpec((1, d), lambda i, j: (0, 0)),
            pl.BlockSpec((None, d, tn), lambda i, j: (idx, 0, j)),
            pl.BlockSpec((d, n_tail), lambda i, j: (0, 0)),
            pl.BlockSpec((1, dh), lambda i, j: (0, 0)),
            pl.BlockSpec((1, dh), lambda i, j: (0, 0)),
        ],
        out_specs=[
            pl.BlockSpec((tm, tn), lambda i, j: (i, jnp.minimum(j, 3 * sb - 1))),
            pl.BlockSpec((tm, tn), lambda i, j: (i, jnp.maximum(j - 3 * sb, 0))),
            pl.BlockSpec((tm, n_tail), lambda i, j: (i, 0)),
        ],
        out_shape=[
            jax.ShapeDtypeStruct((t, 3 * d), BF16),
            jax.ShapeDtypeStruct((t, d), F32),
            jax.ShapeDtypeStruct((t, n_tail), F32),
        ],
        scratch_shapes=[pltpu.VMEM((tm, d), BF16)],
        compiler_params=_params("parallel", "arbitrary"),
        name="proj_odd",
    )(h, g.reshape(1, d), w_all, w_tail, q_g.reshape(1, dh), k_g.reshape(1, dh))


def _out_even_kernel(h_ref, ya_ref, yb_ref, gt_ref, wa_ref, wb_ref, g_ref, o_ref):
    yb = (yb_ref[...].astype(F32) * gt_ref[...]).astype(BF16)
    m = jnp.dot(ya_ref[...], wa_ref[...], preferred_element_type=F32)
    m = m + jnp.dot(yb, wb_ref[...], preferred_element_type=F32)
    o_ref[...] = h_ref[...] + m * _rms_scale(m) * g_ref[...]


def _out_even(h, y_a, y_b, gate, w_out, idx, g, bsz, seq):
    t, d = h.shape
    da = y_a.shape[1]
    db = gate.shape[1]
    tm = _pick_tile(seq, (688, 512, 384, 256, 128, 64, 32, 16))
    nt = seq // tm
    rows = lambda width: pl.BlockSpec((tm, width), lambda b, j: (b * nt + j, 0))
    once = pl.Buffered(1)
    return pl.pallas_call(
        _out_even_kernel,
        grid=(bsz, nt),
        in_specs=[
            rows(d), rows(da),
            pl.BlockSpec((tm, db), lambda b, j: (j, b)),
            rows(db),
            pl.BlockSpec((None, da, d), lambda b, j: (idx, 0, 0), pipeline_mode=once),
            pl.BlockSpec((None, db, d), lambda b, j: (idx, da // db, 0), pipeline_mode=once),
            pl.BlockSpec((1, d), lambda b, j: (0, 0)),
        ],
        out_specs=rows(d),
        out_shape=jax.ShapeDtypeStruct((t, d), F32),
        compiler_params=_params("parallel", "parallel"),
        name="out_even",
    )(h, y_a, y_b, gate, w_out, w_out, g.reshape(1, d))


def _out_odd_kernel(h_ref, o_in_ref, og_ref, w_ref, g_ref, o_ref):
    act = (o_in_ref[...].astype(F32) * _sigmoid(og_ref[...])).astype(BF16)
    m = jnp.dot(act, w_ref[...], preferred_element_type=F32)
    o_ref[...] = h_ref[...] + m * _rms_scale(m) * g_ref[...]


def _out_odd(h, o, og, w_out, idx, g):
    t, d = h.shape
    tm = _pick_tile(t, (384, 256, 128, 64, 32, 16))
    return pl.pallas_call(
        _out_odd_kernel,
        grid=(t // tm,),
        in_specs=[
            pl.BlockSpec((tm, d), lambda i: (i, 0)),
            pl.BlockSpec((tm, d), lambda i: (i, 0)),
            pl.BlockSpec((tm, d), lambda i: (i, 0)),
            pl.BlockSpec((None, d, d), lambda i: (idx, 0, 0)),
            pl.BlockSpec((1, d), lambda i: (0, 0)),
        ],
        out_specs=pl.BlockSpec((tm, d), lambda i: (i, 0)),
        out_shape=jax.ShapeDtypeStruct((t, d), F32),
        compiler_params=_params("parallel"),
        name="out_odd",
    )(h, o, og, w_out, g.reshape(1, d))


def _shift_rows(x, n):
    rows = lax.broadcasted_iota(jnp.int32, x.shape, 0)
    return jnp.where(rows < n, 0.0, pltpu.roll(x, n, 0))


def _even_prep_kernel(pl_ref, mul_ref, w0_ref, w2_ref, a0_ref, a2_ref, g2_ref,
                      w_ref, a_ref, g_ref):
    cur = pl_ref[...]
    ul = cur + (_shift_rows(cur, 1) - cur) * mul_ref[...]
    xw = w0_ref[...] + jnp.dot(jnp.tanh(ul).astype(BF16), w2_ref[...], preferred_element_type=F32)
    w_ref[...] = jnp.exp(-math.exp(-0.5) * _sigmoid(xw))
    a_ref[...] = _sigmoid(a0_ref[...] + jnp.dot(ul.astype(BF16), a2_ref[...],
                                                preferred_element_type=F32))
    g_ref[...] = jnp.dot(_sigmoid(ul).astype(BF16), g2_ref[...], preferred_element_type=F32)


def _even_prep(p_tail, mu_tail, w0, w2, a0, a2, g2, bsz, seq):
    d_rwkv = w0.shape[0]
    decay_rank, aaa_rank = w2.shape[0], a2.shape[0]
    n_tail = p_tail.shape[-1]
    tc = LANES
    nc = d_rwkv // tc

    def vec():
        return pl.BlockSpec((1, tc), lambda b, c: (0, c))

    def tail_rows(w, start):
        return jnp.pad(w, ((start, n_tail - start - w.shape[0]), (0, 0))).astype(BF16)

    lowrank = pl.BlockSpec((n_tail, tc), lambda b, c: (0, c))
    tm_spec = pl.BlockSpec((seq, tc), lambda b, c: (0, b * nc + c))
    tm_shape = jax.ShapeDtypeStruct((seq, bsz * d_rwkv), F32)
    return pl.pallas_call(
        _even_prep_kernel,
        grid=(bsz, nc),
        in_specs=[
            pl.BlockSpec((None, seq, n_tail), lambda b, c: (b, 0, 0)),
            pl.BlockSpec((1, n_tail), lambda b, c: (0, 0)),
            vec(), lowrank, vec(), lowrank, lowrank,
        ],
        out_specs=[tm_spec, tm_spec, pl.BlockSpec((None, seq, tc), lambda b, c: (b, 0, c))],
        out_shape=[tm_shape, tm_shape, jax.ShapeDtypeStruct((bsz, seq, d_rwkv), F32)],
        compiler_params=_params("parallel", "arbitrary"),
        name="even_prep",
    )(p_tail.reshape(bsz, seq, n_tail),
      jnp.pad(mu_tail, (0, n_tail - mu_tail.shape[0])).reshape(1, n_tail),
      w0.reshape(1, -1), tail_rows(w2, 0), a0.reshape(1, -1), tail_rows(a2, decay_rank),
      tail_rows(g2, decay_rank + aaa_rank))


SCAN_ROWS = LANES // 2
N_SCAN_OPS = 5


def _swap_rows_channels(x):
    half = x.shape[0]
    t = jnp.concatenate([x, jnp.zeros_like(x)], axis=0).T
    lane = lax.broadcasted_iota(jnp.int32, x.shape, 1)
    return jnp.where(lane < half, t[:half], pltpu.roll(t[half:], half, 1))


def _scan_kernel(r_ref, w_ref, k_ref, v_ref, lr_ref, kkw_ref, kaw_ref, rk_ref, lg_ref, lb_ref,
                 y_ref, s_ref, kk_ref, kb_ref, kf_ref, yp_ref, ops_ref, *, steps, n):
    @pl.when(pl.program_id(1) == 0)
    def _():
        s_ref[...] = jnp.zeros_like(s_ref)

    def stage(t, slot):
        for idx, ref in enumerate((r_ref, w_ref, k_ref, v_ref, lr_ref)):
            ops_ref[slot, idx] = _swap_rows_channels(ref[t])

    def emit(t):
        y_ref[t] = _swap_rows_channels(yp_ref[...]).astype(y_ref.dtype)

    stage(0, 0)
    yp_ref[...] = jnp.zeros_like(yp_ref)

    def step(t, slot):
        stage(jnp.minimum(t + 1, steps - 1), 1 - slot)
        emit(jnp.maximum(t - 1, 0))
        r_t, w_t = ops_ref.at[slot, 0], ops_ref.at[slot, 1]
        kt = ops_ref[slot, 2]
        vt = ops_ref[slot, 3]
        lr = ops_ref[slot, 4]
        kr = kt * kkw_ref[...]
        nrm = jnp.sqrt(jnp.sum(kr * kr, axis=0, keepdims=True))
        kk = kr / jnp.maximum(nrm, 1e-12)
        kf = kt * (1.0 + (lr - 1.0) * kaw_ref[...])
        kk_ref[...] = kk
        kb_ref[...] = kk * lr
        kf_ref[...] = kf

        sa = jnp.zeros_like(vt)
        for k in range(n):
            sa = sa + s_ref[k] * kk_ref[pl.ds(k, 1), :]
        sa = -sa

        y = jnp.zeros_like(vt)
        for k in range(n):
            sk = (s_ref[k] * w_t[pl.ds(k, 1), :] + sa * kb_ref[pl.ds(k, 1), :]
                  + vt * kf_ref[pl.ds(k, 1), :])
            s_ref[k] = sk
            y = y + sk * r_t[pl.ds(k, 1), :]

        mean = jnp.mean(y, axis=0, keepdims=True)
        yc = y - mean
        var = jnp.mean(yc * yc, axis=0, keepdims=True)
        yn = yc * lax.rsqrt(var + LNX_EPS)
        bonus = jnp.sum(r_t[...] * kf * rk_ref[...], axis=0, keepdims=True)
        yp_ref[...] = yn * lg_ref[...] + lb_ref[...] + bonus * vt

    def step_pair(i, carry):
        step(2 * i, 0)
        step(2 * i + 1, 1)
        return carry

    lax.fori_loop(0, steps // 2, step_pair, 0)
    emit(steps - 1)


def _scan(rkv, w, lr, params, heads):
    _, seq, width = rkv.shape
    n = HEAD_RWKV
    rows = width // LANES
    pairs = heads * n // LANES
    assert 2 * n == LANES and rows % SCAN_ROWS == 0 and SCAN_ROWS % pairs == 0
    steps = _pick_tile(seq, (48, 24, 16, 8, 4, 2))
    assert steps % 2 == 0
    lane = jnp.arange(LANES)
    head_of_lane = 2 * ((lane % SCAN_ROWS) % pairs) + lane // SCAN_ROWS
    tiles = [p.reshape(heads, n)[head_of_lane].T for p in params]
    data = pl.BlockSpec((steps, SCAN_ROWS, LANES), lambda g, t: (t, g, 0))
    stacked = lambda s: pl.BlockSpec((None, steps, SCAN_ROWS, LANES), lambda g, t: (s, t, g, 0))
    tile = pl.BlockSpec((n, LANES), lambda g, t: (0, 0))
    rkv4 = rkv.reshape(3, seq, rows, LANES)
    y = pl.pallas_call(
        functools.partial(_scan_kernel, steps=steps, n=n),
        grid=(rows // SCAN_ROWS, seq // steps),
        in_specs=[stacked(0), data, stacked(1), stacked(2), data] + [tile] * len(tiles),
        out_specs=data,
        out_shape=jax.ShapeDtypeStruct((seq, rows, LANES), BF16),
        scratch_shapes=[pltpu.VMEM((n, n, LANES), F32)] + [pltpu.VMEM((n, LANES), F32)] * 4
        + [pltpu.VMEM((2, N_SCAN_OPS, n, LANES), F32)],
        compiler_params=_params("parallel", "arbitrary"),
        name="rwkv_scan",
    )(rkv4, w.reshape(seq, rows, LANES), rkv4, rkv4, lr.reshape(seq, rows, LANES), *tiles)
    return y.reshape(seq, width)


def _cumsum_kernel(f_ref, b_ref, c_ref):
    z = f_ref[...] + b_ref[...]
    c = jnp.minimum(z, 0.0) - jnp.log(1.0 + jnp.exp(-jnp.abs(z)))
    seq = c.shape[0]
    rows = lax.broadcasted_iota(jnp.int32, c.shape, 0)
    shift = 1
    while shift < seq:
        c = c + jnp.where(rows < shift, 0.0, pltpu.roll(c, shift, 0))
        shift *= 2
    c_ref[...] = c


def _forget_cumsum(p_tail, b_f, bsz, seq):
    n_tail = p_tail.shape[-1]
    b = jnp.pad(b_f, (0, n_tail - b_f.shape[0])).reshape(1, n_tail)
    return pl.pallas_call(
        _cumsum_kernel,
        grid=(bsz,),
        in_specs=[pl.BlockSpec((None, seq, n_tail), lambda i: (i, 0, 0)),
                  pl.BlockSpec((1, n_tail), lambda i: (0, 0))],
        out_specs=pl.BlockSpec((None, seq, n_tail), lambda i: (i, 0, 0)),
        out_shape=jax.ShapeDtypeStruct((bsz, seq, n_tail), F32),
        compiler_params=_params("parallel"),
        name="forget_cumsum",
    )(p_tail.reshape(bsz, seq, n_tail), b)


def _fox_kernel(q_ref, k_ref, v_ref, c_ref, cr_ref, o_ref,
                qs_ref, ks_ref, vs_ref, cc_ref, *, seq, seq_pad):
    dh = q_ref.shape[-1]
    if seq_pad > seq:
        pad = jnp.zeros((seq_pad - seq, dh), BF16)
        qs_ref[pl.ds(seq, seq_pad - seq), :] = pad
        ks_ref[pl.ds(seq, seq_pad - seq), :] = pad
        vs_ref[pl.ds(seq, seq_pad - seq), :] = pad
        cc_ref[pl.ds(seq, seq_pad - seq), :] = jnp.zeros((seq_pad - seq, 1), F32)
    c_all = c_ref[...]
    lane = lax.broadcasted_iota(jnp.int32, c_all.shape, 1)
    cc_ref[pl.ds(0, seq), :] = jnp.sum(jnp.where(lane == pl.program_id(1), c_all, 0.0),
                                       axis=-1, keepdims=True)
    qs_ref[pl.ds(0, seq), :] = q_ref[...]
    ks_ref[pl.ds(0, seq), :] = k_ref[...]
    vs_ref[pl.ds(0, seq), :] = v_ref[...]

    def scores(q0, nq):
        return lax.dot_general(qs_ref[pl.ds(q0, nq), :], ks_ref[pl.ds(0, q0 + nq), :],
                               (((1,), (1,)), ((), ())), preferred_element_type=F32)

    chunks = [(q0, min(FOX_Q_ROWS, seq_pad - q0)) for q0 in range(0, seq_pad, FOX_Q_ROWS)]
    qk_next = scores(*chunks[0])
    for ci, (q0, nq) in enumerate(chunks):
        q1 = q0 + nq
        qk = qk_next
        if ci + 1 < len(chunks):
            qk_next = scores(*chunks[ci + 1])
        u = qk - cr_ref[:, pl.ds(0, q1)] * LOG2E
        rows = lax.broadcasted_iota(jnp.int32, (nq, nq), 0)
        cols = lax.broadcasted_iota(jnp.int32, (nq, nq), 1)
        u_diag = jnp.where(rows >= cols, u[:, q0:], -jnp.inf)
        mu = jnp.max(u_diag, axis=-1, keepdims=True)
        if q0 > 0:
            mu = jnp.maximum(mu, jnp.max(u[:, :q0], axis=-1, keepdims=True))
        ct = cc_ref[pl.ds(q0, nq), :] * LOG2E
        shift = ct - (mu + ct)
        e = jnp.exp2(u_diag + shift)
        if q0 > 0:
            e = jnp.concatenate([jnp.exp2(u[:, :q0] + shift), e], axis=1)
        l = jnp.sum(e, axis=-1, keepdims=True)
        ob = jnp.dot(e.astype(BF16), vs_ref[pl.ds(0, q1), :], preferred_element_type=F32) / l
        n_valid = min(q1, seq) - q0
        if n_valid > 0:
            o_ref[pl.ds(q0, n_valid), :] = ob[:n_valid].astype(BF16)


def _fox_attention(qkv, c, bsz, seq, heads):
    dh = HEAD_FOX
    seq_pad = -(-seq // Q_BLOCK) * Q_BLOCK
    pm = qkv.reshape(bsz, seq, -1)
    n_tail = c.shape[-1]
    ct = jnp.pad(c[:, :, :heads].transpose(0, 2, 1), ((0, 0), (0, 0), (0, seq_pad - seq)))
    c_row = ct[:, :, None, :]

    def seg(s):
        return pl.BlockSpec((None, seq, dh), lambda b, h, s=s: (b, 0, s * heads + h))

    return pl.pallas_call(
        functools.partial(_fox_kernel, seq=seq, seq_pad=seq_pad),
        grid=(bsz, heads),
        in_specs=[
            seg(0), seg(1), seg(2),
            pl.BlockSpec((None, seq, n_tail), lambda b, h: (b, 0, 0)),
            pl.BlockSpec((None, None, 1, seq_pad), lambda b, h: (b, h, 0, 0)),
        ],
        out_specs=pl.BlockSpec((None, seq, dh), lambda b, h: (b, 0, h)),
        out_shape=jax.ShapeDtypeStruct((bsz, seq, heads * dh), BF16),
        scratch_shapes=[pltpu.VMEM((seq_pad, dh), BF16)] * 3 + [pltpu.VMEM((seq_pad, 1), F32)],
        compiler_params=_params("parallel", "parallel"),
        name="fox_attention",
    )(pm, pm, pm, c, c_row)


def _pad_cols(w, n):
    return jnp.pad(w, ((0, 0), (0, n - w.shape[1])))


def _round_up(n, m):
    return -(-n // m) * m


def kernel(x, meta, norm_g, ffn_in, ffn_out, e_w_in, e_conv_w, e_mu, e_w0, e_w2, e_a0, e_a2,
           e_g2, e_k_k, e_k_a, e_r_k, e_lnx_g, e_lnx_b, e_w_out, o_w_in, o_b_f, o_q_g, o_k_g,
           o_w_out):
    bsz, seq_x, d = x.shape
    depth = norm_g.shape[0]
    seq = seq_x + meta.shape[0]
    d_conv = e_conv_w.shape[-1]
    d_rwkv = e_w0.shape[-1]
    h_rwkv = d_rwkv // HEAD_RWKV
    h_fox = d // HEAD_FOX
    n_even_main = 3 * d_conv + 3 * d_rwkv
    n_odd_main = 4 * d

    h = jnp.concatenate([jnp.broadcast_to(meta[None].astype(x.dtype), (bsz,) + meta.shape), x], axis=1)
    h = h.reshape(bsz * seq, d)

    ffn_in_b = ffn_in.astype(BF16)
    ffn_out_b = ffn_out.astype(BF16)
    e_w_in_b = e_w_in.astype(BF16)
    e_w_out_b = e_w_out.astype(BF16)
    o_w_in_b = o_w_in.astype(BF16)
    o_w_out_b = o_w_out.astype(BF16)

    def tail_weights(w_in, n_main):
        n_tail = _round_up(w_in.shape[1] - n_main, LANES)
        return _pad_cols(w_in[:, n_main:], n_tail).astype(BF16)

    for l in range(depth):
        g = norm_g[l]
        i = l // 2
        h = _ffn(h, g[0], ffn_in_b, ffn_out_b, g[1], l, 0)
        if l % 2 == 0:
            assert d_conv == d_rwkv
            mu_rkv = e_mu[i][:3 * d_rwkv].reshape(3, 1, d_rwkv)
            y_a, rkv, p_tail = _proj_even(h, g[2], e_w_in_b, i,
                                          tail_weights(e_w_in[i], n_even_main),
                                          e_conv_w[i], mu_rkv, bsz, seq)
            w, lr, gate = _even_prep(p_tail, e_mu[i][3 * d_rwkv:], e_w0[i], e_w2[i], e_a0[i],
                                     e_a2[i], e_g2[i], bsz, seq)
            y_b = _scan(rkv, w, lr,
                        (e_k_k[i], e_k_a[i], e_r_k[i], e_lnx_g[i], e_lnx_b[i]), h_rwkv)
            h = _out_even(h, y_a, y_b, gate.reshape(bsz * seq, d_rwkv), e_w_out_b, i, g[3],
                          bsz, seq)
        else:
            qkv, og, p_tail = _proj_odd(h, g[2], o_w_in_b, i, tail_weights(o_w_in[i], n_odd_main),
                                        o_q_g[i], o_k_g[i])
            c = _forget_cumsum(p_tail, o_b_f[i], bsz, seq)
            o = _fox_attention(qkv, c, bsz, seq, h_fox)
            h = _out_odd(h, o.reshape(bsz * seq, d), og, o_w_out_b, i, g[3])
        h = _ffn(h, g[4], ffn_in_b, ffn_out_b, g[5], l, 1)
    return h.reshape(bsz, seq, d)[:, meta.shape[0]:]
```

```python
import functools
import math

import jax
import jax.numpy as jnp
from jax import lax
from jax.experimental import pallas as pl
from jax.experimental.pallas import tpu as pltpu

F32 = jnp.float32
BF16 = jnp.bfloat16

N_META = 16
EPS = 1e-6
LNX_EPS = 64e-5
HEAD_RWKV = 64
HEAD_FOX = 128
CONV_W = 3
Q_BLOCK = 128
FOX_Q_ROWS = 256
LOG2E = math.log2(math.e)

LANES = 128
VMEM_LIMIT = 56 * 1024 * 1024


def _params(*sem):
    return pltpu.CompilerParams(dimension_semantics=sem, vmem_limit_bytes=VMEM_LIMIT)


def _pick_tile(n, candidates):
    for c in candidates:
        if n % c == 0:
            return c
    return n


def _rms_scale(x):
    return lax.rsqrt(jnp.mean(x * x, axis=-1, keepdims=True) + EPS)


def _sigmoid(x):
    return 1.0 / (1.0 + jnp.exp(-x))


def _ffn_kernel(x_ref, gi_ref, wg_ref, wu_ref, wo_ref, go_ref, o_ref, xn_ref):
    k = pl.program_id(1)
    nk = pl.num_programs(1)

    @pl.when(k == 0)
    def _():
        x = x_ref[...]
        xn_ref[...] = (x * _rms_scale(x) * gi_ref[...]).astype(BF16)
        o_ref[...] = jnp.zeros_like(o_ref)

    xn = xn_ref[...]
    gate = jnp.dot(xn, wg_ref[...], preferred_element_type=F32)
    up = jnp.dot(xn, wu_ref[...], preferred_element_type=F32)
    mid = (gate * _sigmoid(gate) * up).astype(BF16)
    o_ref[...] += jnp.dot(mid, wo_ref[...], preferred_element_type=F32)

    @pl.when(k == nk - 1)
    def _():
        y = o_ref[...]
        o_ref[...] = x_ref[...] + 0.5 * (y * _rms_scale(y) * go_ref[...])


def _ffn(h, g_in, w_in, w_out, g_out, layer, slot):
    t, d = h.shape
    d_ff = w_out.shape[2]
    tm = _pick_tile(t, (768, 512, 384, 256, 128, 64, 32, 16))
    tf = _pick_tile(d_ff, (512, 256, 128))
    nk = d_ff // tf
    return pl.pallas_call(
        _ffn_kernel,
        grid=(t // tm, nk),
        in_specs=[
            pl.BlockSpec((tm, d), lambda i, k: (i, 0)),
            pl.BlockSpec((1, d), lambda i, k: (0, 0)),
            pl.BlockSpec((None, None, d, tf), lambda i, k: (layer, slot, 0, k)),
            pl.BlockSpec((None, None, d, tf), lambda i, k: (layer, slot, 0, nk + k)),
            pl.BlockSpec((None, None, tf, d), lambda i, k: (layer, slot, k, 0)),
            pl.BlockSpec((1, d), lambda i, k: (0, 0)),
        ],
        out_specs=pl.BlockSpec((tm, d), lambda i, k: (i, 0)),
        out_shape=jax.ShapeDtypeStruct((t, d), F32),
        scratch_shapes=[pltpu.VMEM((tm, d), BF16)],
        compiler_params=_params("parallel", "arbitrary"),
        name="ffn",
    )(h, g_in.reshape(1, d), w_in, w_in, w_out, g_out.reshape(1, d))


N_CONV_SEG = 3


def _proj_even_kernel(x_ref, g_ref, wm_ref, wt_ref, cw_ref, mu_ref, ya_ref, rkv_ref, pt_ref,
                      xn_ref, gb_ref, gc_ref, zc_ref, pc_ref, *, tiles_per_seq):
    i = pl.program_id(0)
    j = pl.program_id(1)
    tm = x_ref.shape[0]

    @pl.when(j == 0)
    def _():
        x = x_ref[...]
        xn = (x * _rms_scale(x) * g_ref[...]).astype(BF16)
        xn_ref[...] = xn
        pt_ref[...] = jnp.dot(xn, wt_ref[...], preferred_element_type=F32)

        @pl.when(i % tiles_per_seq == 0)
        def _():
            zc_ref[...] = jnp.zeros_like(zc_ref)
            pc_ref[...] = jnp.zeros_like(pc_ref)

    def project():
        return jnp.dot(xn_ref[...], wm_ref[...], preferred_element_type=F32)

    def shifted(cur, carry, n):
        rows = lax.broadcasted_iota(jnp.int32, cur.shape, 0)
        out = pltpu.roll(cur, n, 0)
        for r in range(n):
            out = jnp.where(rows == r, carry[r:r + 1], out)
        return out

    @pl.when(j == 0)
    def _():
        gb_ref[...] = project()

    @pl.when(j == 1)
    def _():
        gc_ref[...] = project()

    @pl.when(j == 2)
    def _():
        z = gc_ref[...] * project()
        zc = zc_ref[...]
        cw = cw_ref[...]
        conv = (cw[0:1] * shifted(z, zc[0:2], 2) + cw[1:2] * shifted(z, zc[1:2], 1)
                + cw[2:3] * z)
        ya_ref[...] = (gb_ref[...] * conv).astype(BF16)
        zc_ref[0:2] = z[tm - 2:tm]

    @pl.when(j >= N_CONV_SEG)
    def _():
        s = j - N_CONV_SEG
        p = project()
        rkv_ref[...] = p + (shifted(p, pc_ref[s, 0:1], 1) - p) * mu_ref[...]
        pc_ref[s, 0:1] = p[tm - 1:tm]


def _proj_even(h, g, w_all, idx, w_tail, conv_w, mu_rkv, bsz, seq):
    t, d = h.shape
    n_tail = w_tail.shape[1]
    seg = conv_w.shape[1]
    tm = _pick_tile(seq, (688, 512, 384, 256, 128, 64, 32, 16))
    nt = seq // tm
    rkv_seg = lambda j: jnp.maximum(j - N_CONV_SEG, 0)
    return pl.pallas_call(
        functools.partial(_proj_even_kernel, tiles_per_seq=nt),
        grid=(bsz * nt, 2 * N_CONV_SEG),
        in_specs=[
            pl.BlockSpec((tm, d), lambda i, j: (i, 0)),
            pl.BlockSpec((1, d), lambda i, j: (0, 0)),
            pl.BlockSpec((None, d, seg), lambda i, j: (idx, 0, j)),
            pl.BlockSpec((d, n_tail), lambda i, j: (0, 0)),
            pl.BlockSpec((CONV_W, seg), lambda i, j: (0, 0)),
            pl.BlockSpec((None, 1, seg), lambda i, j: (rkv_seg(j), 0, 0)),
        ],
        out_specs=[
            pl.BlockSpec((tm, seg), lambda i, j: (i, 0)),
            pl.BlockSpec((None, tm, seg), lambda i, j: (rkv_seg(j), i % nt, i // nt)),
            pl.BlockSpec((tm, n_tail), lambda i, j: (i, 0)),
        ],
        out_shape=[
            jax.ShapeDtypeStruct((t, seg), BF16),
            jax.ShapeDtypeStruct((N_CONV_SEG, seq, bsz * seg), F32),
            jax.ShapeDtypeStruct((t, n_tail), F32),
        ],
        scratch_shapes=[pltpu.VMEM((tm, d), BF16), pltpu.VMEM((tm, seg), F32),
                        pltpu.VMEM((tm, seg), F32), pltpu.VMEM((8, seg), F32),
                        pltpu.VMEM((N_CONV_SEG, 8, seg), F32)],
        compiler_params=_params("arbitrary", "arbitrary"),
        name="proj_even",
    )(h, g.reshape(1, d), w_all, w_tail, conv_w, mu_rkv)


def _proj_odd_kernel(x_ref, g_ref, wm_ref, wt_ref, qg_ref, kg_ref, qkv_ref, og_ref, pt_ref,
                     xn_ref, *, seg_blocks):
    j = pl.program_id(1)
    dh = qg_ref.shape[-1]

    @pl.when(j == 0)
    def _():
        x = x_ref[...]
        xn = (x * _rms_scale(x) * g_ref[...]).astype(BF16)
        xn_ref[...] = xn
        pt_ref[...] = jnp.dot(xn, wt_ref[...], preferred_element_type=F32)

    def project():
        return jnp.dot(xn_ref[...], wm_ref[...], preferred_element_type=F32)

    def head_norm(acc, gain, scale):
        heads = [acc[:, c:c + dh] for c in range(0, acc.shape[1], dh)]
        return jnp.concatenate([(xh * _rms_scale(xh) * gain * scale).astype(BF16) for xh in heads],
                               axis=1)

    @pl.when(j < seg_blocks)
    def _():
        qkv_ref[...] = head_norm(project(), qg_ref[...], dh ** -0.5 * LOG2E)

    @pl.when((j >= seg_blocks) & (j < 2 * seg_blocks))
    def _():
        qkv_ref[...] = head_norm(project(), kg_ref[...], 1.0)

    @pl.when((j >= 2 * seg_blocks) & (j < 3 * seg_blocks))
    def _():
        qkv_ref[...] = project().astype(BF16)

    @pl.when(j >= 3 * seg_blocks)
    def _():
        og_ref[...] = project()


def _proj_odd(h, g, w_all, idx, w_tail, q_g, k_g):
    t, d = h.shape
    n_tail = w_tail.shape[1]
    dh = q_g.shape[0]
    tm = _pick_tile(t, (768, 512, 384, 256, 128, 64, 32, 16))
    tn = _pick_tile(d, (1024, 512, 256, 128))
    sb = d // tn
    return pl.pallas_call(
        functools.partial(_proj_odd_kernel, seg_blocks=sb),
        grid=(t // tm, 4 * sb),
        in_specs=[
            pl.BlockSpec((tm, d), lambda i, j: (i, 0)),
            pl.BlockSpec((1, d), lambda i, j: (0, 0)),
            pl.BlockSpec((None, d, tn), lambda i, j: (idx, 0, j)),
            pl.BlockSpec((d, n_tail), lambda i, j: (0, 0)),
            pl.BlockSpec((1, dh), lambda i, j: (0, 0)),
            pl.BlockSpec((1, dh), lambda i, j: (0, 0)),
        ],
        out_specs=[
            pl.BlockSpec((tm, tn), lambda i, j: (i, jnp.minimum(j, 3 * sb - 1))),
            pl.BlockSpec((tm, tn), lambda i, j: (i, jnp.maximum(j - 3 * sb, 0))),
            pl.BlockSpec((tm, n_tail), lambda i, j: (i, 0)),
        ],
        out_shape=[
            jax.ShapeDtypeStruct((t, 3 * d), BF16),
            jax.ShapeDtypeStruct((t, d), F32),
            jax.ShapeDtypeStruct((t, n_tail), F32),
        ],
        scratch_shapes=[pltpu.VMEM((tm, d), BF16)],
        compiler_params=_params("parallel", "arbitrary"),
        name="proj_odd",
    )(h, g.reshape(1, d), w_all, w_tail, q_g.reshape(1, dh), k_g.reshape(1, dh))


def _out_even_kernel(h_ref, ya_ref, yb_ref, gt_ref, wa_ref, wb_ref, g_ref, o_ref):
    yb = (yb_ref[...].astype(F32) * gt_ref[...]).astype(BF16)
    m = jnp.dot(ya_ref[...], wa_ref[...], preferred_element_type=F32)
    m = m + jnp.dot(yb, wb_ref[...], preferred_element_type=F32)
    o_ref[...] = h_ref[...] + m * _rms_scale(m) * g_ref[...]


def _out_even(h, y_a, y_b, gate, w_out, idx, g, bsz, seq):
    t, d = h.shape
    da = y_a.shape[1]
    db = gate.shape[1]
    tm = _pick_tile(seq, (688, 512, 384, 256, 128, 64, 32, 16))
    nt = seq // tm
    rows = lambda width: pl.BlockSpec((tm, width), lambda b, j: (b * nt + j, 0))
    once = pl.Buffered(1)
    return pl.pallas_call(
        _out_even_kernel,
        grid=(bsz, nt),
        in_specs=[
            rows(d), rows(da),
            pl.BlockSpec((tm, db), lambda b, j: (j, b)),
            rows(db),
            pl.BlockSpec((None, da, d), lambda b, j: (idx, 0, 0), pipeline_mode=once),
            pl.BlockSpec((None, db, d), lambda b, j: (idx, da // db, 0), pipeline_mode=once),
            pl.BlockSpec((1, d), lambda b, j: (0, 0)),
        ],
        out_specs=rows(d),
        out_shape=jax.ShapeDtypeStruct((t, d), F32),
        compiler_params=_params("parallel", "parallel"),
        name="out_even",
    )(h, y_a, y_b, gate, w_out, w_out, g.reshape(1, d))


def _out_odd_kernel(h_ref, o_in_ref, og_ref, w_ref, g_ref, o_ref):
    act = (o_in_ref[...].astype(F32) * _sigmoid(og_ref[...])).astype(BF16)
    m = jnp.dot(act, w_ref[...], preferred_element_type=F32)
    o_ref[...] = h_ref[...] + m * _rms_scale(m) * g_ref[...]


def _out_odd(h, o, og, w_out, idx, g):
    t, d = h.shape
    tm = _pick_tile(t, (384, 256, 128, 64, 32, 16))
    return pl.pallas_call(
        _out_odd_kernel,
        grid=(t // tm,),
        in_specs=[
            pl.BlockSpec((tm, d), lambda i: (i, 0)),
            pl.BlockSpec((tm, d), lambda i: (i, 0)),
            pl.BlockSpec((tm, d), lambda i: (i, 0)),
            pl.BlockSpec((None, d, d), lambda i: (idx, 0, 0)),
            pl.BlockSpec((1, d), lambda i: (0, 0)),
        ],
        out_specs=pl.BlockSpec((tm, d), lambda i: (i, 0)),
        out_shape=jax.ShapeDtypeStruct((t, d), F32),
        compiler_params=_params("parallel"),
        name="out_odd",
    )(h, o, og, w_out, g.reshape(1, d))


def _shift_rows(x, n):
    rows = lax.broadcasted_iota(jnp.int32, x.shape, 0)
    return jnp.where(rows < n, 0.0, pltpu.roll(x, n, 0))


def _even_prep_kernel(pl_ref, mul_ref, w0_ref, w2_ref, a0_ref, a2_ref, g2_ref,
                      w_ref, a_ref, g_ref):
    cur = pl_ref[...]
    ul = cur + (_shift_rows(cur, 1) - cur) * mul_ref[...]
    xw = w0_ref[...] + jnp.dot(jnp.tanh(ul).astype(BF16), w2_ref[...], preferred_element_type=F32)
    w_ref[...] = jnp.exp(-math.exp(-0.5) * _sigmoid(xw))
    a_ref[...] = _sigmoid(a0_ref[...] + jnp.dot(ul.astype(BF16), a2_ref[...],
                                                preferred_element_type=F32))
    g_ref[...] = jnp.dot(_sigmoid(ul).astype(BF16), g2_ref[...], preferred_element_type=F32)


def _even_prep(p_tail, mu_tail, w0, w2, a0, a2, g2, bsz, seq):
    d_rwkv = w0.shape[0]
    decay_rank, aaa_rank = w2.shape[0], a2.shape[0]
    n_tail = p_tail.shape[-1]
    tc = LANES
    nc = d_rwkv // tc

    def vec():
        return pl.BlockSpec((1, tc), lambda b, c: (0, c))

    def tail_rows(w, start):
        return jnp.pad(w, ((start, n_tail - start - w.shape[0]), (0, 0))).astype(BF16)

    lowrank = pl.BlockSpec((n_tail, tc), lambda b, c: (0, c))
    tm_spec = pl.BlockSpec((seq, tc), lambda b, c: (0, b * nc + c))
    tm_shape = jax.ShapeDtypeStruct((seq, bsz * d_rwkv), F32)
    return pl.pallas_call(
        _even_prep_kernel,
        grid=(bsz, nc),
        in_specs=[
            pl.BlockSpec((None, seq, n_tail), lambda b, c: (b, 0, 0)),
            pl.BlockSpec((1, n_tail), lambda b, c: (0, 0)),
            vec(), lowrank, vec(), lowrank, lowrank,
        ],
        out_specs=[tm_spec, tm_spec, pl.BlockSpec((None, seq, tc), lambda b, c: (b, 0, c))],
        out_shape=[tm_shape, tm_shape, jax.ShapeDtypeStruct((bsz, seq, d_rwkv), F32)],
        compiler_params=_params("parallel", "arbitrary"),
        name="even_prep",
    )(p_tail.reshape(bsz, seq, n_tail),
      jnp.pad(mu_tail, (0, n_tail - mu_tail.shape[0])).reshape(1, n_tail),
      w0.reshape(1, -1), tail_rows(w2, 0), a0.reshape(1, -1), tail_rows(a2, decay_rank),
      tail_rows(g2, decay_rank + aaa_rank))


SCAN_ROWS = LANES // 2
N_SCAN_OPS = 5


def _swap_rows_channels(x0, x1):
    half = x0.shape[0]
    t = jnp.concatenate([x0, x1], axis=0).T
    top, bot = t[:half], t[half:]
    lane = lax.broadcasted_iota(jnp.int32, x0.shape, 1)
    return (jnp.where(lane < half, top, pltpu.roll(bot, half, 1)),
            jnp.where(lane < half, pltpu.roll(top, half, 1), bot))


def _scan_kernel(r_ref, w_ref, k_ref, v_ref, lr_ref, kkw_ref, kaw_ref, rk_ref, lg_ref, lb_ref,
                 y_ref, s_ref, kk_ref, kb_ref, kf_ref, yp_ref, ops_ref, *, steps, n):
    @pl.when(pl.program_id(1) == 0)
    def _():
        s_ref[...] = jnp.zeros_like(s_ref)

    def stage(t, pslot):
        for idx, ref in enumerate((r_ref, w_ref, k_ref, v_ref, lr_ref)):
            ops_ref[pslot, 0, idx], ops_ref[pslot, 1, idx] = _swap_rows_channels(ref[t], ref[t + 1])

    def emit(t, pslot):
        y0, y1 = _swap_rows_channels(yp_ref[pslot, 0], yp_ref[pslot, 1])
        y_ref[t] = y0.astype(y_ref.dtype)
        y_ref[t + 1] = y1.astype(y_ref.dtype)

    stage(0, 0)
    yp_ref[...] = jnp.zeros_like(yp_ref)

    def step(pslot, sub):
        r_t, w_t = ops_ref.at[pslot, sub, 0], ops_ref.at[pslot, sub, 1]
        kt = ops_ref[pslot, sub, 2]
        vt = ops_ref[pslot, sub, 3]
        lr = ops_ref[pslot, sub, 4]
        kr = kt * kkw_ref[...]
        nrm = jnp.sqrt(jnp.sum(kr * kr, axis=0, keepdims=True))
        kk = kr / jnp.maximum(nrm, 1e-12)
        kf = kt * (1.0 + (lr - 1.0) * kaw_ref[...])
        kk_ref[...] = kk
        kb_ref[...] = kk * lr
        kf_ref[...] = kf

        sa = jnp.zeros_like(vt)
        for k in range(n):
            sa = sa + s_ref[k] * kk_ref[pl.ds(k, 1), :]
        sa = -sa

        y = jnp.zeros_like(vt)
        for k in range(n):
            sk = (s_ref[k] * w_t[pl.ds(k, 1), :] + sa * kb_ref[pl.ds(k, 1), :]
                  + vt * kf_ref[pl.ds(k, 1), :])
            s_ref[k] = sk
            y = y + sk * r_t[pl.ds(k, 1), :]

        mean = jnp.mean(y, axis=0, keepdims=True)
        yc = y - mean
        var = jnp.mean(yc * yc, axis=0, keepdims=True)
        yn = yc * lax.rsqrt(var + LNX_EPS)
        bonus = jnp.sum(r_t[...] * kf * rk_ref[...], axis=0, keepdims=True)
        yp_ref[pslot, sub] = yn * lg_ref[...] + lb_ref[...] + bonus * vt

    def two_pairs(i, carry):
        base = 4 * i
        stage(jnp.minimum(base + 2, steps - 2), 1)
        emit(jnp.maximum(base - 2, 0), 1)
        step(0, 0)
        step(0, 1)
        stage(jnp.minimum(base + 4, steps - 2), 0)
        emit(base, 0)
        step(1, 0)
        step(1, 1)
        return carry

    lax.fori_loop(0, steps // 4, two_pairs, 0)
    emit(steps - 2, 1)


def _scan(rkv, w, lr, params, heads):
    _, seq, width = rkv.shape
    n = HEAD_RWKV
    rows = width // LANES
    pairs = heads * n // LANES
    assert 2 * n == LANES and rows % SCAN_ROWS == 0 and SCAN_ROWS % pairs == 0
    steps = _pick_tile(seq, (48, 24, 16, 8, 4))
    assert steps % 4 == 0
    lane = jnp.arange(LANES)
    head_of_lane = 2 * ((lane % SCAN_ROWS) % pairs) + lane // SCAN_ROWS
    tiles = [p.reshape(heads, n)[head_of_lane].T for p in params]
    data = pl.BlockSpec((steps, SCAN_ROWS, LANES), lambda g, t: (t, g, 0))
    stacked = lambda s: pl.BlockSpec((None, steps, SCAN_ROWS, LANES), lambda g, t: (s, t, g, 0))
    tile = pl.BlockSpec((n, LANES), lambda g, t: (0, 0))
    rkv4 = rkv.reshape(3, seq, rows, LANES)
    y = pl.pallas_call(
        functools.partial(_scan_kernel, steps=steps, n=n),
        grid=(rows // SCAN_ROWS, seq // steps),
        in_specs=[stacked(0), data, stacked(1), stacked(2), data] + [tile] * len(tiles),
        out_specs=data,
        out_shape=jax.ShapeDtypeStruct((seq, rows, LANES), BF16),
        scratch_shapes=[pltpu.VMEM((n, n, LANES), F32)] + [pltpu.VMEM((n, LANES), F32)] * 3
        + [pltpu.VMEM((2, 2, n, LANES), F32), pltpu.VMEM((2, 2, N_SCAN_OPS, n, LANES), F32)],
        compiler_params=_params("parallel", "arbitrary"),
        name="rwkv_scan",
    )(rkv4, w.reshape(seq, rows, LANES), rkv4, rkv4, lr.reshape(seq, rows, LANES), *tiles)
    return y.reshape(seq, width)


def _cumsum_kernel(f_ref, b_ref, c_ref):
    z = f_ref[...] + b_ref[...]
    c = jnp.minimum(z, 0.0) - jnp.log(1.0 + jnp.exp(-jnp.abs(z)))
    seq = c.shape[0]
    rows = lax.broadcasted_iota(jnp.int32, c.shape, 0)
    shift = 1
    while shift < seq:
        c = c + jnp.where(rows < shift, 0.0, pltpu.roll(c, shift, 0))
        shift *= 2
    c_ref[...] = c


def _forget_cumsum(p_tail, b_f, bsz, seq):
    n_tail = p_tail.shape[-1]
    b = jnp.pad(b_f, (0, n_tail - b_f.shape[0])).reshape(1, n_tail)
    return pl.pallas_call(
        _cumsum_kernel,
        grid=(bsz,),
        in_specs=[pl.BlockSpec((None, seq, n_tail), lambda i: (i, 0, 0)),
                  pl.BlockSpec((1, n_tail), lambda i: (0, 0))],
        out_specs=pl.BlockSpec((None, seq, n_tail), lambda i: (i, 0, 0)),
        out_shape=jax.ShapeDtypeStruct((bsz, seq, n_tail), F32),
        compiler_params=_params("parallel"),
        name="forget_cumsum",
    )(p_tail.reshape(bsz, seq, n_tail), b)


def _fox_kernel(q_ref, k_ref, v_ref, c_ref, cr_ref, o_ref,
                qs_ref, ks_ref, vs_ref, cc_ref, *, seq, seq_pad):
    dh = q_ref.shape[-1]
    if seq_pad > seq:
        pad = jnp.zeros((seq_pad - seq, dh), BF16)
        qs_ref[pl.ds(seq, seq_pad - seq), :] = pad
        ks_ref[pl.ds(seq, seq_pad - seq), :] = pad
        vs_ref[pl.ds(seq, seq_pad - seq), :] = pad
        cc_ref[pl.ds(seq, seq_pad - seq), :] = jnp.zeros((seq_pad - seq, 1), F32)
    c_all = c_ref[...]
    lane = lax.broadcasted_iota(jnp.int32, c_all.shape, 1)
    cc_ref[pl.ds(0, seq), :] = jnp.sum(jnp.where(lane == pl.program_id(1), c_all, 0.0),
                                       axis=-1, keepdims=True)
    qs_ref[pl.ds(0, seq), :] = q_ref[...]
    ks_ref[pl.ds(0, seq), :] = k_ref[...]
    vs_ref[pl.ds(0, seq), :] = v_ref[...]

    def scores(q0, nq):
        return lax.dot_general(qs_ref[pl.ds(q0, nq), :], ks_ref[pl.ds(0, q0 + nq), :],
                               (((1,), (1,)), ((), ())), preferred_element_type=F32)

    chunks = [(q0, min(FOX_Q_ROWS, seq_pad - q0)) for q0 in range(0, seq_pad, FOX_Q_ROWS)]
    qk_next = scores(*chunks[0])
    for ci, (q0, nq) in enumerate(chunks):
        q1 = q0 + nq
        qk = qk_next
        if ci + 1 < len(chunks):
            qk_next = scores(*chunks[ci + 1])
        u = qk - cr_ref[:, pl.ds(0, q1)] * LOG2E
        rows = lax.broadcasted_iota(jnp.int32, (nq, nq), 0)
        cols = lax.broadcasted_iota(jnp.int32, (nq, nq), 1)
        u_diag = jnp.where(rows >= cols, u[:, q0:], -jnp.inf)
        mu = jnp.max(u_diag, axis=-1, keepdims=True)
        if q0 > 0:
            mu = jnp.maximum(mu, jnp.max(u[:, :q0], axis=-1, keepdims=True))
        ct = cc_ref[pl.ds(q0, nq), :] * LOG2E
        shift = ct - (mu + ct)
        e = jnp.exp2(u_diag + shift)
        if q0 > 0:
            e = jnp.concatenate([jnp.exp2(u[:, :q0] + shift), e], axis=1)
        l = jnp.sum(e, axis=-1, keepdims=True)
        ob = jnp.dot(e.astype(BF16), vs_ref[pl.ds(0, q1), :], preferred_element_type=F32) / l
        n_valid = min(q1, seq) - q0
        if n_valid > 0:
            o_ref[pl.ds(q0, n_valid), :] = ob[:n_valid].astype(BF16)


def _fox_attention(qkv, c, bsz, seq, heads):
    dh = HEAD_FOX
    seq_pad = -(-seq // Q_BLOCK) * Q_BLOCK
    pm = qkv.reshape(bsz, seq, -1)
    n_tail = c.shape[-1]
    ct = jnp.pad(c[:, :, :heads].transpose(0, 2, 1), ((0, 0), (0, 0), (0, seq_pad - seq)))
    c_row = ct[:, :, None, :]

    def seg(s):
        return pl.BlockSpec((None, seq, dh), lambda b, h, s=s: (b, 0, s * heads + h))

    return pl.pallas_call(
        functools.partial(_fox_kernel, seq=seq, seq_pad=seq_pad),
        grid=(bsz, heads),
        in_specs=[
            seg(0), seg(1), seg(2),
            pl.BlockSpec((None, seq, n_tail), lambda b, h: (b, 0, 0)),
            pl.BlockSpec((None, None, 1, seq_pad), lambda b, h: (b, h, 0, 0)),
        ],
        out_specs=pl.BlockSpec((None, seq, dh), lambda b, h: (b, 0, h)),
        out_shape=jax.ShapeDtypeStruct((bsz, seq, heads * dh), BF16),
        scratch_shapes=[pltpu.VMEM((seq_pad, dh), BF16)] * 3 + [pltpu.VMEM((seq_pad, 1), F32)],
        compiler_params=_params("parallel", "parallel"),
        name="fox_attention",
    )(pm, pm, pm, c, c_row)


def _pad_cols(w, n):
    return jnp.pad(w, ((0, 0), (0, n - w.shape[1])))


def _round_up(n, m):
    return -(-n // m) * m


def kernel(x, meta, norm_g, ffn_in, ffn_out, e_w_in, e_conv_w, e_mu, e_w0, e_w2, e_a0, e_a2,
           e_g2, e_k_k, e_k_a, e_r_k, e_lnx_g, e_lnx_b, e_w_out, o_w_in, o_b_f, o_q_g, o_k_g,
           o_w_out):
    bsz, seq_x, d = x.shape
    depth = norm_g.shape[0]
    seq = seq_x + meta.shape[0]
    d_conv = e_conv_w.shape[-1]
    d_rwkv = e_w0.shape[-1]
    h_rwkv = d_rwkv // HEAD_RWKV
    h_fox = d // HEAD_FOX
    n_even_main = 3 * d_conv + 3 * d_rwkv
    n_odd_main = 4 * d

    h = jnp.concatenate([jnp.broadcast_to(meta[None].astype(x.dtype), (bsz,) + meta.shape), x], axis=1)
    h = h.reshape(bsz * seq, d)

    ffn_in_b = ffn_in.astype(BF16)
    ffn_out_b = ffn_out.astype(BF16)
    e_w_in_b = e_w_in.astype(BF16)
    e_w_out_b = e_w_out.astype(BF16)
    o_w_in_b = o_w_in.astype(BF16)
    o_w_out_b = o_w_out.astype(BF16)

    def tail_weights(w_in, n_main):
        n_tail = _round_up(w_in.shape[1] - n_main, LANES)
        return _pad_cols(w_in[:, n_main:], n_tail).astype(BF16)

    for l in range(depth):
        g = norm_g[l]
        i = l // 2
        h = _ffn(h, g[0], ffn_in_b, ffn_out_b, g[1], l, 0)
        if l % 2 == 0:
            assert d_conv == d_rwkv
            mu_rkv = e_mu[i][:3 * d_rwkv].reshape(3, 1, d_rwkv)
            y_a, rkv, p_tail = _proj_even(h, g[2], e_w_in_b, i,
                                          tail_weights(e_w_in[i], n_even_main),
                                          e_conv_w[i], mu_rkv, bsz, seq)
            w, lr, gate = _even_prep(p_tail, e_mu[i][3 * d_rwkv:], e_w0[i], e_w2[i], e_a0[i],
                                     e_a2[i], e_g2[i], bsz, seq)
            y_b = _scan(rkv, w, lr,
                        (e_k_k[i], e_k_a[i], e_r_k[i], e_lnx_g[i], e_lnx_b[i]), h_rwkv)
            h = _out_even(h, y_a, y_b, gate.reshape(bsz * seq, d_rwkv), e_w_out_b, i, g[3],
                          bsz, seq)
        else:
            qkv, og, p_tail = _proj_odd(h, g[2], o_w_in_b, i, tail_weights(o_w_in[i], n_odd_main),
                                        o_q_g[i], o_k_g[i])
            c = _forget_cumsum(p_tail, o_b_f[i], bsz, seq)
            o = _fox_attention(qkv, c, bsz, seq, h_fox)
            h = _out_odd(h, o.reshape(bsz * seq, d), og, o_w_out_b, i, g[3])
        h = _ffn(h, g[4], ffn_in_b, ffn_out_b, g[5], l, 1)
    return h.reshape(bsz, seq, d)[:, meta.shape[0]:]
```

```python
import functools
import math

import jax
import jax.numpy as jnp
from jax import lax
from jax.experimental import pallas as pl
from jax.experimental.pallas import tpu as pltpu

F32 = jnp.float32
BF16 = jnp.bfloat16

N_META = 16
EPS = 1e-6
LNX_EPS = 64e-5
HEAD_RWKV = 64
HEAD_FOX = 128
CONV_W = 3
Q_BLOCK = 128
FOX_Q_ROWS = 256
LOG2E = math.log2(math.e)

LANES = 128
VMEM_LIMIT = 56 * 1024 * 1024


def _params(*sem):
    return pltpu.CompilerParams(dimension_semantics=sem, vmem_limit_bytes=VMEM_LIMIT)


def _pick_tile(n, candidates):
    for c in candidates:
        if n % c == 0:
            return c
    return n


def _rms_scale(x):
    return lax.rsqrt(jnp.mean(x * x, axis=-1, keepdims=True) + EPS)


def _sigmoid(x):
    return 1.0 / (1.0 + jnp.exp(-x))


def _ffn_kernel(x_ref, gi_ref, wg_ref, wu_ref, wo_ref, go_ref, o_ref, xn_ref):
    k = pl.program_id(1)
    nk = pl.num_programs(1)

    @pl.when(k == 0)
    def _():
        x = x_ref[...]
        xn_ref[...] = (x * _rms_scale(x) * gi_ref[...]).astype(BF16)
        o_ref[...] = jnp.zeros_like(o_ref)

    xn = xn_ref[...]
    gate = jnp.dot(xn, wg_ref[...], preferred_element_type=F32)
    up = jnp.dot(xn, wu_ref[...], preferred_element_type=F32)
    mid = (gate * _sigmoid(gate) * up).astype(BF16)
    o_ref[...] += jnp.dot(mid, wo_ref[...], preferred_element_type=F32)

    @pl.when(k == nk - 1)
    def _():
        y = o_ref[...]
        o_ref[...] = x_ref[...] + 0.5 * (y * _rms_scale(y) * go_ref[...])


def _ffn(h, g_in, w_in, w_out, g_out, layer, slot):
    t, d = h.shape
    d_ff = w_out.shape[2]
    tm = _pick_tile(t, (768, 512, 384, 256, 128, 64, 32, 16))
    tf = _pick_tile(d_ff, (512, 256, 128))
    nk = d_ff // tf
    return pl.pallas_call(
        _ffn_kernel,
        grid=(t // tm, nk),
        in_specs=[
            pl.BlockSpec((tm, d), lambda i, k: (i, 0)),
            pl.BlockSpec((1, d), lambda i, k: (0, 0)),
            pl.BlockSpec((None, None, d, tf), lambda i, k: (layer, slot, 0, k)),
            pl.BlockSpec((None, None, d, tf), lambda i, k: (layer, slot, 0, nk + k)),
            pl.BlockSpec((None, None, tf, d), lambda i, k: (layer, slot, k, 0)),
            pl.BlockSpec((1, d), lambda i, k: (0, 0)),
        ],
        out_specs=pl.BlockSpec((tm, d), lambda i, k: (i, 0)),
        out_shape=jax.ShapeDtypeStruct((t, d), F32),
        scratch_shapes=[pltpu.VMEM((tm, d), BF16)],
        compiler_params=_params("parallel", "arbitrary"),
        name="ffn",
    )(h, g_in.reshape(1, d), w_in, w_in, w_out, g_out.reshape(1, d))


N_CONV_SEG = 3


def _proj_even_kernel(x_ref, g_ref, wm_ref, wt_ref, cw_ref, mu_ref, ya_ref, rkv_ref, pt_ref,
                      xn_ref, gb_ref, gc_ref, zc_ref, pc_ref, *, tiles_per_seq):
    i = pl.program_id(0)
    j = pl.program_id(1)
    tm = x_ref.shape[0]

    @pl.when(j == 0)
    def _():
        x = x_ref[...]
        xn = (x * _rms_scale(x) * g_ref[...]).astype(BF16)
        xn_ref[...] = xn
        pt_ref[...] = jnp.dot(xn, wt_ref[...], preferred_element_type=F32)

        @pl.when(i % tiles_per_seq == 0)
        def _():
            zc_ref[...] = jnp.zeros_like(zc_ref)
            pc_ref[...] = jnp.zeros_like(pc_ref)

    def project():
        return jnp.dot(xn_ref[...], wm_ref[...], preferred_element_type=F32)

    def shifted(cur, carry, n):
        rows = lax.broadcasted_iota(jnp.int32, cur.shape, 0)
        out = pltpu.roll(cur, n, 0)
        for r in range(n):
            out = jnp.where(rows == r, carry[r:r + 1], out)
        return out

    @pl.when(j == 0)
    def _():
        gb_ref[...] = project()

    @pl.when(j == 1)
    def _():
        gc_ref[...] = project()

    @pl.when(j == 2)
    def _():
        z = gc_ref[...] * project()
        zc = zc_ref[...]
        cw = cw_ref[...]
        conv = (cw[0:1] * shifted(z, zc[0:2], 2) + cw[1:2] * shifted(z, zc[1:2], 1)
                + cw[2:3] * z)
        ya_ref[...] = (gb_ref[...] * conv).astype(BF16)
        zc_ref[0:2] = z[tm - 2:tm]

    @pl.when(j >= N_CONV_SEG)
    def _():
        s = j - N_CONV_SEG
        p = project()
        rkv_ref[...] = p + (shifted(p, pc_ref[s, 0:1], 1) - p) * mu_ref[...]
        pc_ref[s, 0:1] = p[tm - 1:tm]


def _proj_even(h, g, w_all, idx, w_tail, conv_w, mu_rkv, bsz, seq):
    t, d = h.shape
    n_tail = w_tail.shape[1]
    seg = conv_w.shape[1]
    tm = _pick_tile(seq, (688, 512, 384, 256, 128, 64, 32, 16))
    nt = seq // tm
    rkv_seg = lambda j: jnp.maximum(j - N_CONV_SEG, 0)
    return pl.pallas_call(
        functools.partial(_proj_even_kernel, tiles_per_seq=nt),
        grid=(bsz * nt, 2 * N_CONV_SEG),
        in_specs=[
            pl.BlockSpec((tm, d), lambda i, j: (i, 0)),
            pl.BlockSpec((1, d), lambda i, j: (0, 0)),
            pl.BlockSpec((None, d, seg), lambda i, j: (idx, 0, j)),
            pl.BlockSpec((d, n_tail), lambda i, j: (0, 0)),
            pl.BlockSpec((CONV_W, seg), lambda i, j: (0, 0)),
            pl.BlockSpec((None, 1, seg), lambda i, j: (rkv_seg(j), 0, 0)),
        ],
        out_specs=[
            pl.BlockSpec((tm, seg), lambda i, j: (i, 0)),
            pl.BlockSpec((None, tm, seg), lambda i, j: (rkv_seg(j), i % nt, i // nt)),
            pl.BlockSpec((tm, n_tail), lambda i, j: (i, 0)),
        ],
        out_shape=[
            jax.ShapeDtypeStruct((t, seg), BF16),
            jax.ShapeDtypeStruct((N_CONV_SEG, seq, bsz * seg), F32),
            jax.ShapeDtypeStruct((t, n_tail), F32),
        ],
        scratch_shapes=[pltpu.VMEM((tm, d), BF16), pltpu.VMEM((tm, seg), F32),
                        pltpu.VMEM((tm, seg), F32), pltpu.VMEM((8, seg), F32),
                        pltpu.VMEM((N_CONV_SEG, 8, seg), F32)],
        compiler_params=_params("arbitrary", "arbitrary"),
        name="proj_even",
    )(h, g.reshape(1, d), w_all, w_tail, conv_w, mu_rkv)


def _proj_odd_kernel(x_ref, g_ref, wm_ref, wt_ref, qg_ref, kg_ref, qkv_ref, og_ref, pt_ref,
                     xn_ref, *, seg_blocks):
    j = pl.program_id(1)
    dh = qg_ref.shape[-1]

    @pl.when(j == 0)
    def _():
        x = x_ref[...]
        xn = (x * _rms_scale(x) * g_ref[...]).astype(BF16)
        xn_ref[...] = xn
        pt_ref[...] = jnp.dot(xn, wt_ref[...], preferred_element_type=F32)

    def project():
        return jnp.dot(xn_ref[...], wm_ref[...], preferred_element_type=F32)

    def head_norm(acc, gain, scale):
        heads = [acc[:, c:c + dh] for c in range(0, acc.shape[1], dh)]
        return jnp.concatenate([(xh * _rms_scale(xh) * gain * scale).astype(BF16) for xh in heads],
                               axis=1)

    @pl.when(j < seg_blocks)
    def _():
        qkv_ref[...] = head_norm(project(), qg_ref[...], dh ** -0.5 * LOG2E)

    @pl.when((j >= seg_blocks) & (j < 2 * seg_blocks))
    def _():
        qkv_ref[...] = head_norm(project(), kg_ref[...], 1.0)

    @pl.when((j >= 2 * seg_blocks) & (j < 3 * seg_blocks))
    def _():
        qkv_ref[...] = project().astype(BF16)

    @pl.when(j >= 3 * seg_blocks)
    def _():
        og_ref[...] = project()


def _proj_odd(h, g, w_all, idx, w_tail, q_g, k_g):
    t, d = h.shape
    n_tail = w_tail.shape[1]
    dh = q_g.shape[0]
    tm = _pick_tile(t, (768, 512, 384, 256, 128, 64, 32, 16))
    tn = _pick_tile(d, (1024, 512, 256, 128))
    sb = d // tn
    return pl.pallas_call(
        functools.partial(_proj_odd_kernel, seg_blocks=sb),
        grid=(t // tm, 4 * sb),
        in_specs=[
            pl.BlockSpec((tm, d), lambda i, j: (i, 0)),
            pl.BlockSpec((1, d), lambda i, j: (0, 0)),
            pl.BlockSpec((None, d, tn), lambda i, j: (idx, 0, j)),
            pl.BlockSpec((d, n_tail), lambda i, j: (0, 0)),
            pl.BlockSpec((1, dh), lambda i, j: (0, 0)),
            pl.BlockSpec((1, dh), lambda i, j: (0, 0)),
        ],
        out_specs=[
            pl.BlockSpec((tm, tn), lambda i, j: (i, jnp.minimum(j, 3 * sb - 1))),
            pl.BlockSpec((tm, tn), lambda i, j: (i, jnp.maximum(j - 3 * sb, 0))),
            pl.BlockSpec((tm, n_tail), lambda i, j: (i, 0)),
        ],
        out_shape=[
            jax.ShapeDtypeStruct((t, 3 * d), BF16),
            jax.ShapeDtypeStruct((t, d), F32),
            jax.ShapeDtypeStruct((t, n_tail), F32),
        ],
        scratch_shapes=[pltpu.VMEM((tm, d), BF16)],
        compiler_params=_params("parallel", "arbitrary"),
        name="proj_odd",
    )(h, g.reshape(1, d), w_all, w_tail, q_g.reshape(1, dh), k_g.reshape(1, dh))


def _out_even_kernel(h_ref, ya_ref, yb_ref, gt_ref, wa_ref, wb_ref, g_ref, o_ref):
    yb = (yb_ref[...].astype(F32) * gt_ref[...]).astype(BF16)
    m = jnp.dot(ya_ref[...], wa_ref[...], preferred_element_type=F32)
    m = m + jnp.dot(yb, wb_ref[...], preferred_element_type=F32)
    o_ref[...] = h_ref[...] + m * _rms_scale(m) * g_ref[...]


def _out_even(h, y_a, y_b, gate, w_out, idx, g, bsz, seq):
    t, d = h.shape
    da = y_a.shape[1]
    db = gate.shape[1]
    tm = _pick_tile(seq, (688, 512, 384, 256, 128, 64, 32, 16))
    nt = seq // tm
    rows = lambda width: pl.BlockSpec((tm, width), lambda b, j: (b * nt + j, 0))
    once = pl.Buffered(1)
    return pl.pallas_call(
        _out_even_kernel,
        grid=(bsz, nt),
        in_specs=[
            rows(d), rows(da),
            pl.BlockSpec((tm, db), lambda b, j: (j, b)),
            rows(db),
            pl.BlockSpec((None, da, d), lambda b, j: (idx, 0, 0), pipeline_mode=once),
            pl.BlockSpec((None, db, d), lambda b, j: (idx, da // db, 0), pipeline_mode=once),
            pl.BlockSpec((1, d), lambda b, j: (0, 0)),
        ],
        out_specs=rows(d),
        out_shape=jax.ShapeDtypeStruct((t, d), F32),
        compiler_params=_params("parallel", "parallel"),
        name="out_even",
    )(h, y_a, y_b, gate, w_out, w_out, g.reshape(1, d))


def _out_odd_kernel(h_ref, o_in_ref, og_ref, w_ref, g_ref, o_ref):
    act = (o_in_ref[...].astype(F32) * _sigmoid(og_ref[...])).astype(BF16)
    m = jnp.dot(act, w_ref[...], preferred_element_type=F32)
    o_ref[...] = h_ref[...] + m * _rms_scale(m) * g_ref[...]


def _out_odd(h, o, og, w_out, idx, g):
    t, d = h.shape
    tm = _pick_tile(t, (384, 256, 128, 64, 32, 16))
    return pl.pallas_call(
        _out_odd_kernel,
        grid=(t // tm,),
        in_specs=[
            pl.BlockSpec((tm, d), lambda i: (i, 0)),
            pl.BlockSpec((tm, d), lambda i: (i, 0)),
            pl.BlockSpec((tm, d), lambda i: (i, 0)),
            pl.BlockSpec((None, d, d), lambda i: (idx, 0, 0)),
            pl.BlockSpec((1, d), lambda i: (0, 0)),
        ],
        out_specs=pl.BlockSpec((tm, d), lambda i: (i, 0)),
        out_shape=jax.ShapeDtypeStruct((t, d), F32),
        compiler_params=_params("parallel"),
        name="out_odd",
    )(h, o, og, w_out, g.reshape(1, d))


def _shift_rows(x, n):
    rows = lax.broadcasted_iota(jnp.int32, x.shape, 0)
    return jnp.where(rows < n, 0.0, pltpu.roll(x, n, 0))


def _even_prep_kernel(pl_ref, mul_ref, w0_ref, w2_ref, a0_ref, a2_ref, g2_ref,
                      w_ref, a_ref, g_ref):
    cur = pl_ref[...]
    ul = cur + (_shift_rows(cur, 1) - cur) * mul_ref[...]
    xw = w0_ref[...] + jnp.dot(jnp.tanh(ul).astype(BF16), w2_ref[...], preferred_element_type=F32)
    w_ref[...] = jnp.exp(-math.exp(-0.5) * _sigmoid(xw))
    a_ref[...] = _sigmoid(a0_ref[...] + jnp.dot(ul.astype(BF16), a2_ref[...],
                                                preferred_element_type=F32))
    g_ref[...] = jnp.dot(_sigmoid(ul).astype(BF16), g2_ref[...], preferred_element_type=F32)


def _even_prep(p_tail, mu_tail, w0, w2, a0, a2, g2, bsz, seq):
    d_rwkv = w0.shape[0]
    decay_rank, aaa_rank = w2.shape[0], a2.shape[0]
    n_tail = p_tail.shape[-1]
    tc = _pick_tile(d_rwkv, (4 * LANES, 2 * LANES, LANES))
    nc = d_rwkv // tc

    def vec():
        return pl.BlockSpec((1, tc), lambda b, c: (0, c))

    def tail_rows(w, start):
        return jnp.pad(w, ((start, n_tail - start - w.shape[0]), (0, 0))).astype(BF16)

    lowrank = pl.BlockSpec((n_tail, tc), lambda b, c: (0, c))
    tm_spec = pl.BlockSpec((seq, tc), lambda b, c: (0, b * nc + c))
    tm_shape = jax.ShapeDtypeStruct((seq, bsz * d_rwkv), F32)
    return pl.pallas_call(
        _even_prep_kernel,
        grid=(bsz, nc),
        in_specs=[
            pl.BlockSpec((None, seq, n_tail), lambda b, c: (b, 0, 0)),
            pl.BlockSpec((1, n_tail), lambda b, c: (0, 0)),
            vec(), lowrank, vec(), lowrank, lowrank,
        ],
        out_specs=[tm_spec, tm_spec, pl.BlockSpec((None, seq, tc), lambda b, c: (b, 0, c))],
        out_shape=[tm_shape, tm_shape, jax.ShapeDtypeStruct((bsz, seq, d_rwkv), F32)],
        compiler_params=_params("parallel", "arbitrary"),
        name="even_prep",
    )(p_tail.reshape(bsz, seq, n_tail),
      jnp.pad(mu_tail, (0, n_tail - mu_tail.shape[0])).reshape(1, n_tail),
      w0.reshape(1, -1), tail_rows(w2, 0), a0.reshape(1, -1), tail_rows(a2, decay_rank),
      tail_rows(g2, decay_rank + aaa_rank))


SCAN_ROWS = LANES // 2
N_SCAN_OPS = 5


def _swap_rows_channels(x0, x1):
    half = x0.shape[0]
    t = jnp.concatenate([x0, x1], axis=0).T
    top, bot = t[:half], t[half:]
    lane = lax.broadcasted_iota(jnp.int32, x0.shape, 1)
    return (jnp.where(lane < half, top, pltpu.roll(bot, half, 1)),
            jnp.where(lane < half, pltpu.roll(top, half, 1), bot))


def _scan_kernel(r_ref, w_ref, k_ref, v_ref, lr_ref, kkw_ref, kaw_ref, rk_ref, lg_ref, lb_ref,
                 y_ref, s_ref, kk_ref, kb_ref, kf_ref, yp_ref, ops_ref, *, steps, n):
    @pl.when(pl.program_id(1) == 0)
    def _():
        s_ref[...] = jnp.zeros_like(s_ref)

    def stage(t, pslot):
        for idx, ref in enumerate((r_ref, w_ref, k_ref, v_ref, lr_ref)):
            ops_ref[pslot, 0, idx], ops_ref[pslot, 1, idx] = _swap_rows_channels(ref[t], ref[t + 1])

    def emit(t, pslot):
        y0, y1 = _swap_rows_channels(yp_ref[pslot, 0], yp_ref[pslot, 1])
        y_ref[t] = y0.astype(y_ref.dtype)
        y_ref[t + 1] = y1.astype(y_ref.dtype)

    stage(0, 0)
    yp_ref[...] = jnp.zeros_like(yp_ref)

    def step(pslot, sub):
        r_t, w_t = ops_ref.at[pslot, sub, 0], ops_ref.at[pslot, sub, 1]
        kt = ops_ref[pslot, sub, 2]
        vt = ops_ref[pslot, sub, 3]
        lr = ops_ref[pslot, sub, 4]
        kr = kt * kkw_ref[...]
        nrm = jnp.sqrt(jnp.sum(kr * kr, axis=0, keepdims=True))
        kk = kr / jnp.maximum(nrm, 1e-12)
        kf = kt * (1.0 + (lr - 1.0) * kaw_ref[...])
        kk_ref[...] = kk
        kb_ref[...] = kk * lr
        kf_ref[...] = kf

        sa = jnp.zeros_like(vt)
        for k in range(n):
            sa = sa + s_ref[k] * kk_ref[pl.ds(k, 1), :]
        sa = -sa

        y = jnp.zeros_like(vt)
        for k in range(n):
            sk = (s_ref[k] * w_t[pl.ds(k, 1), :] + sa * kb_ref[pl.ds(k, 1), :]
                  + vt * kf_ref[pl.ds(k, 1), :])
            s_ref[k] = sk
            y = y + sk * r_t[pl.ds(k, 1), :]

        mean = jnp.mean(y, axis=0, keepdims=True)
        yc = y - mean
        var = jnp.mean(yc * yc, axis=0, keepdims=True)
        yn = yc * lax.rsqrt(var + LNX_EPS)
        bonus = jnp.sum(r_t[...] * kf * rk_ref[...], axis=0, keepdims=True)
        yp_ref[pslot, sub] = yn * lg_ref[...] + lb_ref[...] + bonus * vt

    def two_pairs(i, carry):
        base = 4 * i
        stage(jnp.minimum(base + 2, steps - 2), 1)
        emit(jnp.maximum(base - 2, 0), 1)
        step(0, 0)
        step(0, 1)
        stage(jnp.minimum(base + 4, steps - 2), 0)
        emit(base, 0)
        step(1, 0)
        step(1, 1)
        return carry

    lax.fori_loop(0, steps // 4, two_pairs, 0)
    emit(steps - 2, 1)


def _scan(rkv, w, lr, params, heads):
    _, seq, width = rkv.shape
    n = HEAD_RWKV
    rows = width // LANES
    pairs = heads * n // LANES
    assert 2 * n == LANES and rows % SCAN_ROWS == 0 and SCAN_ROWS % pairs == 0
    steps = _pick_tile(seq, (48, 24, 16, 8, 4))
    assert steps % 4 == 0
    lane = jnp.arange(LANES)
    head_of_lane = 2 * ((lane % SCAN_ROWS) % pairs) + lane // SCAN_ROWS
    tiles = [p.reshape(heads, n)[head_of_lane].T for p in params]
    data = pl.BlockSpec((steps, SCAN_ROWS, LANES), lambda g, t: (t, g, 0))
    stacked = lambda s: pl.BlockSpec((None, steps, SCAN_ROWS, LANES), lambda g, t: (s, t, g, 0))
    tile = pl.BlockSpec((n, LANES), lambda g, t: (0, 0))
    rkv4 = rkv.reshape(3, seq, rows, LANES)
    y = pl.pallas_call(
        functools.partial(_scan_kernel, steps=steps, n=n),
        grid=(rows // SCAN_ROWS, seq // steps),
        in_specs=[stacked(0), data, stacked(1), stacked(2), data] + [tile] * len(tiles),
        out_specs=data,
        out_shape=jax.ShapeDtypeStruct((seq, rows, LANES), BF16),
        scratch_shapes=[pltpu.VMEM((n, n, LANES), F32)] + [pltpu.VMEM((n, LANES), F32)] * 3
        + [pltpu.VMEM((2, 2, n, LANES), F32), pltpu.VMEM((2, 2, N_SCAN_OPS, n, LANES), F32)],
        compiler_params=_params("parallel", "arbitrary"),
        name="rwkv_scan",
    )(rkv4, w.reshape(seq, rows, LANES), rkv4, rkv4, lr.reshape(seq, rows, LANES), *tiles)
    return y.reshape(seq, width)


def _cumsum_kernel(f_ref, b_ref, c_ref):
    z = f_ref[...] + b_ref[...]
    c = jnp.minimum(z, 0.0) - jnp.log(1.0 + jnp.exp(-jnp.abs(z)))
    seq = c.shape[0]
    rows = lax.broadcasted_iota(jnp.int32, c.shape, 0)
    shift = 1
    while shift < seq:
        c = c + jnp.where(rows < shift, 0.0, pltpu.roll(c, shift, 0))
        shift *= 2
    c_ref[...] = c


def _forget_cumsum(p_tail, b_f, bsz, seq):
    n_tail = p_tail.shape[-1]
    b = jnp.pad(b_f, (0, n_tail - b_f.shape[0])).reshape(1, n_tail)
    return pl.pallas_call(
        _cumsum_kernel,
        grid=(bsz,),
        in_specs=[pl.BlockSpec((None, seq, n_tail), lambda i: (i, 0, 0)),
                  pl.BlockSpec((1, n_tail), lambda i: (0, 0))],
        out_specs=pl.BlockSpec((None, seq, n_tail), lambda i: (i, 0, 0)),
        out_shape=jax.ShapeDtypeStruct((bsz, seq, n_tail), F32),
        compiler_params=_params("parallel"),
        name="forget_cumsum",
    )(p_tail.reshape(bsz, seq, n_tail), b)


def _fox_kernel(q_ref, k_ref, v_ref, c_ref, cr_ref, o_ref,
                qs_ref, ks_ref, vs_ref, cc_ref, *, seq, seq_pad):
    dh = q_ref.shape[-1]
    if seq_pad > seq:
        pad = jnp.zeros((seq_pad - seq, dh), BF16)
        qs_ref[pl.ds(seq, seq_pad - seq), :] = pad
        ks_ref[pl.ds(seq, seq_pad - seq), :] = pad
        vs_ref[pl.ds(seq, seq_pad - seq), :] = pad
        cc_ref[pl.ds(seq, seq_pad - seq), :] = jnp.zeros((seq_pad - seq, 1), F32)
    c_all = c_ref[...]
    lane = lax.broadcasted_iota(jnp.int32, c_all.shape, 1)
    cc_ref[pl.ds(0, seq), :] = jnp.sum(jnp.where(lane == pl.program_id(1), c_all, 0.0),
                                       axis=-1, keepdims=True)
    qs_ref[pl.ds(0, seq), :] = q_ref[...]
    ks_ref[pl.ds(0, seq), :] = k_ref[...]
    vs_ref[pl.ds(0, seq), :] = v_ref[...]

    def scores(q0, nq):
        return lax.dot_general(qs_ref[pl.ds(q0, nq), :], ks_ref[pl.ds(0, q0 + nq), :],
                               (((1,), (1,)), ((), ())), preferred_element_type=F32)

    chunks = [(q0, min(FOX_Q_ROWS, seq_pad - q0)) for q0 in range(0, seq_pad, FOX_Q_ROWS)]
    qk_next = scores(*chunks[0])
    for ci, (q0, nq) in enumerate(chunks):
        q1 = q0 + nq
        qk = qk_next
        if ci + 1 < len(chunks):
            qk_next = scores(*chunks[ci + 1])
        u = qk - cr_ref[:, pl.ds(0, q1)] * LOG2E
        rows = lax.broadcasted_iota(jnp.int32, (nq, nq), 0)
        cols = lax.broadcasted_iota(jnp.int32, (nq, nq), 1)
        u_diag = jnp.where(rows >= cols, u[:, q0:], -jnp.inf)
        mu = jnp.max(u_diag, axis=-1, keepdims=True)
        if q0 > 0:
            mu = jnp.maximum(mu, jnp.max(u[:, :q0], axis=-1, keepdims=True))
        ct = cc_ref[pl.ds(q0, nq), :] * LOG2E
        shift = ct - (mu + ct)
        e = jnp.exp2(u_diag + shift)
        if q0 > 0:
            e = jnp.concatenate([jnp.exp2(u[:, :q0] + shift), e], axis=1)
        l = jnp.sum(e, axis=-1, keepdims=True)
        ob = jnp.dot(e.astype(BF16), vs_ref[pl.ds(0, q1), :], preferred_element_type=F32) / l
        n_valid = min(q1, seq) - q0
        if n_valid > 0:
            o_ref[pl.ds(q0, n_valid), :] = ob[:n_valid].astype(BF16)


def _fox_attention(qkv, c, bsz, seq, heads):
    dh = HEAD_FOX
    seq_pad = -(-seq // Q_BLOCK) * Q_BLOCK
    pm = qkv.reshape(bsz, seq, -1)
    n_tail = c.shape[-1]
    ct = jnp.pad(c[:, :, :heads].transpose(0, 2, 1), ((0, 0), (0, 0), (0, seq_pad - seq)))
    c_row = ct[:, :, None, :]

    def seg(s):
        return pl.BlockSpec((None, seq, dh), lambda b, h, s=s: (b, 0, s * heads + h))

    return pl.pallas_call(
        functools.partial(_fox_kernel, seq=seq, seq_pad=seq_pad),
        grid=(bsz, heads),
        in_specs=[
            seg(0), seg(1), seg(2),
            pl.BlockSpec((None, seq, n_tail), lambda b, h: (b, 0, 0)),
            pl.BlockSpec((None, None, 1, seq_pad), lambda b, h: (b, h, 0, 0)),
        ],
        out_specs=pl.BlockSpec((None, seq, dh), lambda b, h: (b, 0, h)),
        out_shape=jax.ShapeDtypeStruct((bsz, seq, heads * dh), BF16),
        scratch_shapes=[pltpu.VMEM((seq_pad, dh), BF16)] * 3 + [pltpu.VMEM((seq_pad, 1), F32)],
        compiler_params=_params("parallel", "parallel"),
        name="fox_attention",
    )(pm, pm, pm, c, c_row)


def _pad_cols(w, n):
    return jnp.pad(w, ((0, 0), (0, n - w.shape[1])))


def _round_up(n, m):
    return -(-n // m) * m


def kernel(x, meta, norm_g, ffn_in, ffn_out, e_w_in, e_conv_w, e_mu, e_w0, e_w2, e_a0, e_a2,
           e_g2, e_k_k, e_k_a, e_r_k, e_lnx_g, e_lnx_b, e_w_out, o_w_in, o_b_f, o_q_g, o_k_g,
           o_w_out):
    bsz, seq_x, d = x.shape
    depth = norm_g.shape[0]
    seq = seq_x + meta.shape[0]
    d_conv = e_conv_w.shape[-1]
    d_rwkv = e_w0.shape[-1]
    h_rwkv = d_rwkv // HEAD_RWKV
    h_fox = d // HEAD_FOX
    n_even_main = 3 * d_conv + 3 * d_rwkv
    n_odd_main = 4 * d

    h = jnp.concatenate([jnp.broadcast_to(meta[None].astype(x.dtype), (bsz,) + meta.shape), x], axis=1)
    h = h.reshape(bsz * seq, d)

    ffn_in_b = ffn_in.astype(BF16)
    ffn_out_b = ffn_out.astype(BF16)
    e_w_in_b = e_w_in.astype(BF16)
    e_w_out_b = e_w_out.astype(BF16)
    o_w_in_b = o_w_in.astype(BF16)
    o_w_out_b = o_w_out.astype(BF16)

    def tail_weights(w_in, n_main):
        n_tail = _round_up(w_in.shape[1] - n_main, LANES)
        return _pad_cols(w_in[:, n_main:], n_tail).astype(BF16)

    for l in range(depth):
        g = norm_g[l]
        i = l // 2
        h = _ffn(h, g[0], ffn_in_b, ffn_out_b, g[1], l, 0)
        if l % 2 == 0:
            assert d_conv == d_rwkv
            mu_rkv = e_mu[i][:3 * d_rwkv].reshape(3, 1, d_rwkv)
            y_a, rkv, p_tail = _proj_even(h, g[2], e_w_in_b, i,
                                          tail_weights(e_w_in[i], n_even_main),
                                          e_conv_w[i], mu_rkv, bsz, seq)
            w, lr, gate = _even_prep(p_tail, e_mu[i][3 * d_rwkv:], e_w0[i], e_w2[i], e_a0[i],
                                     e_a2[i], e_g2[i], bsz, seq)
            y_b = _scan(rkv, w, lr,
                        (e_k_k[i], e_k_a[i], e_r_k[i], e_lnx_g[i], e_lnx_b[i]), h_rwkv)
            h = _out_even(h, y_a, y_b, gate.reshape(bsz * seq, d_rwkv), e_w_out_b, i, g[3],
                          bsz, seq)
        else:
            qkv, og, p_tail = _proj_odd(h, g[2], o_w_in_b, i, tail_weights(o_w_in[i], n_odd_main),
                                        o_q_g[i], o_k_g[i])
            c = _forget_cumsum(p_tail, o_b_f[i], bsz, seq)
            o = _fox_attention(qkv, c, bsz, seq, h_fox)
            h = _out_odd(h, o.reshape(bsz * seq, d), og, o_w_out_b, i, g[3])
        h = _ffn(h, g[4], ffn_in_b, ffn_out_b, g[5], l, 1)
    return h.reshape(bsz, seq, d)[:, meta.shape[0]:]
```

```python
import functools
import math

import jax
import jax.numpy as jnp
from jax import lax
from jax.experimental import pallas as pl
from jax.experimental.pallas import tpu as pltpu

F32 = jnp.float32
BF16 = jnp.bfloat16

N_META = 16
EPS = 1e-6
LNX_EPS = 64e-5
HEAD_RWKV = 64
HEAD_FOX = 128
CONV_W = 3
Q_BLOCK = 128
FOX_Q_ROWS = 256
LOG2E = math.log2(math.e)

LANES = 128
VMEM_LIMIT = 56 * 1024 * 1024


def _params(*sem):
    return pltpu.CompilerParams(dimension_semantics=sem, vmem_limit_bytes=VMEM_LIMIT)


def _pick_tile(n, candidates):
    for c in candidates:
        if n % c == 0:
            return c
    return n


def _rms_scale(x):
    return lax.rsqrt(jnp.mean(x * x, axis=-1, keepdims=True) + EPS)


def _sigmoid(x):
    return 1.0 / (1.0 + jnp.exp(-x))


def _ffn_kernel(x_ref, gi_ref, wg_ref, wu_ref, wo_ref, go_ref, o_ref, xn_ref):
    k = pl.program_id(1)
    nk = pl.num_programs(1)

    @pl.when(k == 0)
    def _():
        x = x_ref[...]
        xn_ref[...] = (x * _rms_scale(x) * gi_ref[...]).astype(BF16)
        o_ref[...] = jnp.zeros_like(o_ref)

    xn = xn_ref[...]
    gate = jnp.dot(xn, wg_ref[...], preferred_element_type=F32)
    up = jnp.dot(xn, wu_ref[...], preferred_element_type=F32)
    mid = (gate * _sigmoid(gate) * up).astype(BF16)
    o_ref[...] += jnp.dot(mid, wo_ref[...], preferred_element_type=F32)

    @pl.when(k == nk - 1)
    def _():
        y = o_ref[...]
        o_ref[...] = x_ref[...] + 0.5 * (y * _rms_scale(y) * go_ref[...])


def _ffn(h, g_in, w_in, w_out, g_out, layer, slot):
    t, d = h.shape
    d_ff = w_out.shape[2]
    tm = _pick_tile(t, (768, 512, 384, 256, 128, 64, 32, 16))
    tf = _pick_tile(d_ff, (512, 256, 128))
    nk = d_ff // tf
    return pl.pallas_call(
        _ffn_kernel,
        grid=(t // tm, nk),
        in_specs=[
            pl.BlockSpec((tm, d), lambda i, k: (i, 0)),
            pl.BlockSpec((1, d), lambda i, k: (0, 0)),
            pl.BlockSpec((None, None, d, tf), lambda i, k: (layer, slot, 0, k)),
            pl.BlockSpec((None, None, d, tf), lambda i, k: (layer, slot, 0, nk + k)),
            pl.BlockSpec((None, None, tf, d), lambda i, k: (layer, slot, k, 0)),
            pl.BlockSpec((1, d), lambda i, k: (0, 0)),
        ],
        out_specs=pl.BlockSpec((tm, d), lambda i, k: (i, 0)),
        out_shape=jax.ShapeDtypeStruct((t, d), F32),
        scratch_shapes=[pltpu.VMEM((tm, d), BF16)],
        compiler_params=_params("parallel", "arbitrary"),
        name="ffn",
    )(h, g_in.reshape(1, d), w_in, w_in, w_out, g_out.reshape(1, d))


N_CONV_SEG = 3


def _proj_even_kernel(x_ref, g_ref, wm_ref, wt_ref, cw_ref, mu_ref, ya_ref, rkv_ref, pt_ref,
                      xn_ref, gb_ref, gc_ref, zc_ref, pc_ref, *, tiles_per_seq):
    i = pl.program_id(0)
    j = pl.program_id(1)
    tm = x_ref.shape[0]

    @pl.when(j == 0)
    def _():
        x = x_ref[...]
        xn = (x * _rms_scale(x) * g_ref[...]).astype(BF16)
        xn_ref[...] = xn
        pt_ref[...] = jnp.dot(xn, wt_ref[...], preferred_element_type=F32)

        @pl.when(i % tiles_per_seq == 0)
        def _():
            zc_ref[...] = jnp.zeros_like(zc_ref)
            pc_ref[...] = jnp.zeros_like(pc_ref)

    def project():
        return jnp.dot(xn_ref[...], wm_ref[...], preferred_element_type=F32)

    def shifted(cur, carry, n):
        rows = lax.broadcasted_iota(jnp.int32, cur.shape, 0)
        out = pltpu.roll(cur, n, 0)
        for r in range(n):
            out = jnp.where(rows == r, carry[r:r + 1], out)
        return out

    @pl.when(j == 0)
    def _():
        gb_ref[...] = project()

    @pl.when(j == 1)
    def _():
        gc_ref[...] = project()

    @pl.when(j == 2)
    def _():
        z = gc_ref[...] * project()
        zc = zc_ref[...]
        cw = cw_ref[...]
        conv = (cw[0:1] * shifted(z, zc[0:2], 2) + cw[1:2] * shifted(z, zc[1:2], 1)
                + cw[2:3] * z)
        ya_ref[...] = (gb_ref[...] * conv).astype(BF16)
        zc_ref[0:2] = z[tm - 2:tm]

    @pl.when(j >= N_CONV_SEG)
    def _():
        s = j - N_CONV_SEG
        p = project()
        rkv_ref[...] = p + (shifted(p, pc_ref[s, 0:1], 1) - p) * mu_ref[...]
        pc_ref[s, 0:1] = p[tm - 1:tm]


def _proj_even(h, g, w_all, idx, w_tail, conv_w, mu_rkv, bsz, seq):
    t, d = h.shape
    n_tail = w_tail.shape[1]
    seg = conv_w.shape[1]
    tm = _pick_tile(seq, (688, 512, 384, 256, 128, 64, 32, 16))
    nt = seq // tm
    rkv_seg = lambda j: jnp.maximum(j - N_CONV_SEG, 0)
    return pl.pallas_call(
        functools.partial(_proj_even_kernel, tiles_per_seq=nt),
        grid=(bsz * nt, 2 * N_CONV_SEG),
        in_specs=[
            pl.BlockSpec((tm, d), lambda i, j: (i, 0)),
            pl.BlockSpec((1, d), lambda i, j: (0, 0)),
            pl.BlockSpec((None, d, seg), lambda i, j: (idx, 0, j)),
            pl.BlockSpec((d, n_tail), lambda i, j: (0, 0)),
            pl.BlockSpec((CONV_W, seg), lambda i, j: (0, 0)),
            pl.BlockSpec((None, 1, seg), lambda i, j: (rkv_seg(j), 0, 0)),
        ],
        out_specs=[
            pl.BlockSpec((tm, seg), lambda i, j: (i, 0)),
            pl.BlockSpec((None, tm, seg), lambda i, j: (rkv_seg(j), i % nt, i // nt)),
            pl.BlockSpec((tm, n_tail), lambda i, j: (i, 0)),
        ],
        out_shape=[
            jax.ShapeDtypeStruct((t, seg), BF16),
            jax.ShapeDtypeStruct((N_CONV_SEG, seq, bsz * seg), F32),
            jax.ShapeDtypeStruct((t, n_tail), F32),
        ],
        scratch_shapes=[pltpu.VMEM((tm, d), BF16), pltpu.VMEM((tm, seg), F32),
                        pltpu.VMEM((tm, seg), F32), pltpu.VMEM((8, seg), F32),
                        pltpu.VMEM((N_CONV_SEG, 8, seg), F32)],
        compiler_params=_params("arbitrary", "arbitrary"),
        name="proj_even",
    )(h, g.reshape(1, d), w_all, w_tail, conv_w, mu_rkv)


def _proj_odd_kernel(x_ref, g_ref, wm_ref, wt_ref, qg_ref, kg_ref, qkv_ref, og_ref, pt_ref,
                     xn_ref, *, seg_blocks):
    j = pl.program_id(1)
    dh = qg_ref.shape[-1]

    @pl.when(j == 0)
    def _():
        x = x_ref[...]
        xn = (x * _rms_scale(x) * g_ref[...]).astype(BF16)
        xn_ref[...] = xn
        pt_ref[...] = jnp.dot(xn, wt_ref[...], preferred_element_type=F32)

    def project():
        return jnp.dot(xn_ref[...], wm_ref[...], preferred_element_type=F32)

    def head_norm(acc, gain, scale):
        heads = [acc[:, c:c + dh] for c in range(0, acc.shape[1], dh)]
        return jnp.concatenate([(xh * _rms_scale(xh) * gain * scale).astype(BF16) for xh in heads],
                               axis=1)

    @pl.when(j < seg_blocks)
    def _():
        qkv_ref[...] = head_norm(project(), qg_ref[...], dh ** -0.5 * LOG2E)

    @pl.when((j >= seg_blocks) & (j < 2 * seg_blocks))
    def _():
        qkv_ref[...] = head_norm(project(), kg_ref[...], 1.0)

    @pl.when((j >= 2 * seg_blocks) & (j < 3 * seg_blocks))
    def _():
        qkv_ref[...] = project().astype(BF16)

    @pl.when(j >= 3 * seg_blocks)
    def _():
        og_ref[...] = project()


def _proj_odd(h, g, w_all, idx, w_tail, q_g, k_g):
    t, d = h.shape
    n_tail = w_tail.shape[1]
    dh = q_g.shape[0]
    tm = _pick_tile(t, (768, 512, 384, 256, 128, 64, 32, 16))
    tn = _pick_tile(d, (1024, 512, 256, 128))
    sb = d // tn
    return pl.pallas_call(
        functools.partial(_proj_odd_kernel, seg_blocks=sb),
        grid=(t // tm, 4 * sb),
        in_specs=[
            pl.BlockSpec((tm, d), lambda i, j: (i, 0)),
            pl.BlockSpec((1, d), lambda i, j: (0, 0)),
            pl.BlockSpec((None, d, tn), lambda i, j: (idx, 0, j)),
            pl.BlockSpec((d, n_tail), lambda i, j: (0, 0)),
            pl.BlockSpec((1, dh), lambda i, j: (0, 0)),
            pl.BlockSpec((1, dh), lambda i, j: (0, 0)),
        ],
        out_specs=[
            pl.BlockSpec((tm, tn), lambda i, j: (i, jnp.minimum(j, 3 * sb - 1))),
            pl.BlockSpec((tm, tn), lambda i, j: (i, jnp.maximum(j - 3 * sb, 0))),
            pl.BlockSpec((tm, n_tail), lambda i, j: (i, 0)),
        ],
        out_shape=[
            jax.ShapeDtypeStruct((t, 3 * d), BF16),
            jax.ShapeDtypeStruct((t, d), F32),
            jax.ShapeDtypeStruct((t, n_tail), F32),
        ],
        scratch_shapes=[pltpu.VMEM((tm, d), BF16)],
        compiler_params=_params("parallel", "arbitrary"),
        name="proj_odd",
    )(h, g.reshape(1, d), w_all, w_tail, q_g.reshape(1, dh), k_g.reshape(1, dh))


def _out_even_kernel(h_ref, ya_ref, yb_ref, gt_ref, wa_ref, wb_ref, g_ref, o_ref):
    yb = (yb_ref[...].astype(F32) * gt_ref[...]).astype(BF16)
    m = jnp.dot(ya_ref[...], wa_ref[...], preferred_element_type=F32)
    m = m + jnp.dot(yb, wb_ref[...], preferred_element_type=F32)
    o_ref[...] = h_ref[...] + m * _rms_scale(m) * g_ref[...]


def _out_even(h, y_a, y_b, gate, w_out, idx, g, bsz, seq):
    t, d = h.shape
    da = y_a.shape[1]
    db = gate.shape[1]
    tm = _pick_tile(seq, (688, 512, 384, 256, 128, 64, 32, 16))
    nt = seq // tm
    rows = lambda width: pl.BlockSpec((tm, width), lambda b, j: (b * nt + j, 0))
    once = pl.Buffered(1)
    return pl.pallas_call(
        _out_even_kernel,
        grid=(bsz, nt),
        in_specs=[
            rows(d), rows(da),
            pl.BlockSpec((tm, db), lambda b, j: (j, b)),
            rows(db),
            pl.BlockSpec((None, da, d), lambda b, j: (idx, 0, 0), pipeline_mode=once),
            pl.BlockSpec((None, db, d), lambda b, j: (idx, da // db, 0), pipeline_mode=once),
            pl.BlockSpec((1, d), lambda b, j: (0, 0)),
        ],
        out_specs=rows(d),
        out_shape=jax.ShapeDtypeStruct((t, d), F32),
        compiler_params=_params("parallel", "parallel"),
        name="out_even",
    )(h, y_a, y_b, gate, w_out, w_out, g.reshape(1, d))


def _out_odd_kernel(h_ref, o_in_ref, og_ref, w_ref, g_ref, o_ref):
    act = (o_in_ref[...].astype(F32) * _sigmoid(og_ref[...])).astype(BF16)
    m = jnp.dot(act, w_ref[...], preferred_element_type=F32)
    o_ref[...] = h_ref[...] + m * _rms_scale(m) * g_ref[...]


def _out_odd(h, o, og, w_out, idx, g):
    t, d = h.shape
    tm = _pick_tile(t, (688, 384, 256, 128, 64, 32, 16))
    return pl.pallas_call(
        _out_odd_kernel,
        grid=(t // tm,),
        in_specs=[
            pl.BlockSpec((tm, d), lambda i: (i, 0)),
            pl.BlockSpec((tm, d), lambda i: (i, 0)),
            pl.BlockSpec((tm, d), lambda i: (i, 0)),
            pl.BlockSpec((None, d, d), lambda i: (idx, 0, 0), pipeline_mode=pl.Buffered(1)),
            pl.BlockSpec((1, d), lambda i: (0, 0)),
        ],
        out_specs=pl.BlockSpec((tm, d), lambda i: (i, 0)),
        out_shape=jax.ShapeDtypeStruct((t, d), F32),
        compiler_params=_params("parallel"),
        name="out_odd",
    )(h, o, og, w_out, g.reshape(1, d))


def _shift_rows(x, n):
    rows = lax.broadcasted_iota(jnp.int32, x.shape, 0)
    return jnp.where(rows < n, 0.0, pltpu.roll(x, n, 0))


def _even_prep_kernel(pl_ref, mul_ref, w0_ref, w2_ref, a0_ref, a2_ref, g2_ref,
                      w_ref, a_ref, g_ref):
    cur = pl_ref[...]
    ul = cur + (_shift_rows(cur, 1) - cur) * mul_ref[...]
    xw = w0_ref[...] + jnp.dot(jnp.tanh(ul).astype(BF16), w2_ref[...], preferred_element_type=F32)
    w_ref[...] = jnp.exp(-math.exp(-0.5) * _sigmoid(xw))
    a_ref[...] = _sigmoid(a0_ref[...] + jnp.dot(ul.astype(BF16), a2_ref[...],
                                                preferred_element_type=F32))
    g_ref[...] = jnp.dot(_sigmoid(ul).astype(BF16), g2_ref[...], preferred_element_type=F32)


def _even_prep(p_tail, mu_tail, w0, w2, a0, a2, g2, bsz, seq):
    d_rwkv = w0.shape[0]
    decay_rank, aaa_rank = w2.shape[0], a2.shape[0]
    n_tail = p_tail.shape[-1]
    tc = _pick_tile(d_rwkv, (4 * LANES, 2 * LANES, LANES))
    nc = d_rwkv // tc

    def vec():
        return pl.BlockSpec((1, tc), lambda b, c: (0, c))

    def tail_rows(w, start):
        return jnp.pad(w, ((start, n_tail - start - w.shape[0]), (0, 0))).astype(BF16)

    lowrank = pl.BlockSpec((n_tail, tc), lambda b, c: (0, c))
    tm_spec = pl.BlockSpec((seq, tc), lambda b, c: (0, b * nc + c))
    tm_shape = jax.ShapeDtypeStruct((seq, bsz * d_rwkv), F32)
    return pl.pallas_call(
        _even_prep_kernel,
        grid=(bsz, nc),
        in_specs=[
            pl.BlockSpec((None, seq, n_tail), lambda b, c: (b, 0, 0)),
            pl.BlockSpec((1, n_tail), lambda b, c: (0, 0)),
            vec(), lowrank, vec(), lowrank, lowrank,
        ],
        out_specs=[tm_spec, tm_spec, pl.BlockSpec((None, seq, tc), lambda b, c: (b, 0, c))],
        out_shape=[tm_shape, tm_shape, jax.ShapeDtypeStruct((bsz, seq, d_rwkv), F32)],
        compiler_params=_params("parallel", "arbitrary"),
        name="even_prep",
    )(p_tail.reshape(bsz, seq, n_tail),
      jnp.pad(mu_tail, (0, n_tail - mu_tail.shape[0])).reshape(1, n_tail),
      w0.reshape(1, -1), tail_rows(w2, 0), a0.reshape(1, -1), tail_rows(a2, decay_rank),
      tail_rows(g2, decay_rank + aaa_rank))


SCAN_ROWS = LANES // 2
N_SCAN_OPS = 5


def _swap_rows_channels(x0, x1):
    half = x0.shape[0]
    t = jnp.concatenate([x0, x1], axis=0).T
    top, bot = t[:half], t[half:]
    lane = lax.broadcasted_iota(jnp.int32, x0.shape, 1)
    return (jnp.where(lane < half, top, pltpu.roll(bot, half, 1)),
            jnp.where(lane < half, pltpu.roll(top, half, 1), bot))


def _scan_kernel(r_ref, w_ref, k_ref, v_ref, lr_ref, kkw_ref, kaw_ref, rk_ref, lg_ref, lb_ref,
                 y_ref, s_ref, kk_ref, kb_ref, kf_ref, yp_ref, ops_ref, *, steps, n):
    @pl.when(pl.program_id(1) == 0)
    def _():
        s_ref[...] = jnp.zeros_like(s_ref)

    def stage(t, pslot):
        for idx, ref in enumerate((r_ref, w_ref, k_ref, v_ref, lr_ref)):
            ops_ref[pslot, 0, idx], ops_ref[pslot, 1, idx] = _swap_rows_channels(ref[t], ref[t + 1])

    def emit(t, pslot):
        y0, y1 = _swap_rows_channels(yp_ref[pslot, 0], yp_ref[pslot, 1])
        y_ref[t] = y0.astype(y_ref.dtype)
        y_ref[t + 1] = y1.astype(y_ref.dtype)

    stage(0, 0)
    yp_ref[...] = jnp.zeros_like(yp_ref)

    def step(pslot, sub):
        r_t, w_t = ops_ref.at[pslot, sub, 0], ops_ref.at[pslot, sub, 1]
        kt = ops_ref[pslot, sub, 2]
        vt = ops_ref[pslot, sub, 3]
        lr = ops_ref[pslot, sub, 4]
        kr = kt * kkw_ref[...]
        nrm = jnp.sqrt(jnp.sum(kr * kr, axis=0, keepdims=True))
        kk = kr / jnp.maximum(nrm, 1e-12)
        kf = kt * (1.0 + (lr - 1.0) * kaw_ref[...])
        kk_ref[...] = kk
        kb_ref[...] = kk * lr
        kf_ref[...] = kf

        sa = jnp.zeros_like(vt)
        for k in range(n):
            sa = sa + s_ref[k] * kk_ref[pl.ds(k, 1), :]
        sa = -sa

        y = jnp.zeros_like(vt)
        for k in range(n):
            sk = (s_ref[k] * w_t[pl.ds(k, 1), :] + sa * kb_ref[pl.ds(k, 1), :]
                  + vt * kf_ref[pl.ds(k, 1), :])
            s_ref[k] = sk
            y = y + sk * r_t[pl.ds(k, 1), :]

        mean = jnp.mean(y, axis=0, keepdims=True)
        yc = y - mean
        var = jnp.mean(yc * yc, axis=0, keepdims=True)
        yn = yc * lax.rsqrt(var + LNX_EPS)
        bonus = jnp.sum(r_t[...] * kf * rk_ref[...], axis=0, keepdims=True)
        yp_ref[pslot, sub] = yn * lg_ref[...] + lb_ref[...] + bonus * vt

    def two_pairs(i, carry):
        base = 4 * i
        stage(jnp.minimum(base + 2, steps - 2), 1)
        emit(jnp.maximum(base - 2, 0), 1)
        step(0, 0)
        step(0, 1)
        stage(jnp.minimum(base + 4, steps - 2), 0)
        emit(base, 0)
        step(1, 0)
        step(1, 1)
        return carry

    lax.fori_loop(0, steps // 4, two_pairs, 0)
    emit(steps - 2, 1)


def _scan(rkv, w, lr, params, heads):
    _, seq, width = rkv.shape
    n = HEAD_RWKV
    rows = width // LANES
    pairs = heads * n // LANES
    assert 2 * n == LANES and rows % SCAN_ROWS == 0 and SCAN_ROWS % pairs == 0
    steps = _pick_tile(seq, (48, 24, 16, 8, 4))
    assert steps % 4 == 0
    lane = jnp.arange(LANES)
    head_of_lane = 2 * ((lane % SCAN_ROWS) % pairs) + lane // SCAN_ROWS
    tiles = [p.reshape(heads, n)[head_of_lane].T for p in params]
    data = pl.BlockSpec((steps, SCAN_ROWS, LANES), lambda g, t: (t, g, 0))
    stacked = lambda s: pl.BlockSpec((None, steps, SCAN_ROWS, LANES), lambda g, t: (s, t, g, 0))
    tile = pl.BlockSpec((n, LANES), lambda g, t: (0, 0))
    rkv4 = rkv.reshape(3, seq, rows, LANES)
    y = pl.pallas_call(
        functools.partial(_scan_kernel, steps=steps, n=n),
        grid=(rows // SCAN_ROWS, seq // steps),
        in_specs=[stacked(0), data, stacked(1), stacked(2), data] + [tile] * len(tiles),
        out_specs=data,
        out_shape=jax.ShapeDtypeStruct((seq, rows, LANES), BF16),
        scratch_shapes=[pltpu.VMEM((n, n, LANES), F32)] + [pltpu.VMEM((n, LANES), F32)] * 3
        + [pltpu.VMEM((2, 2, n, LANES), F32), pltpu.VMEM((2, 2, N_SCAN_OPS, n, LANES), F32)],
        compiler_params=_params("parallel", "arbitrary"),
        name="rwkv_scan",
    )(rkv4, w.reshape(seq, rows, LANES), rkv4, rkv4, lr.reshape(seq, rows, LANES), *tiles)
    return y.reshape(seq, width)


def _cumsum_kernel(f_ref, b_ref, c_ref):
    z = f_ref[...] + b_ref[...]
    c = jnp.minimum(z, 0.0) - jnp.log(1.0 + jnp.exp(-jnp.abs(z)))
    seq = c.shape[0]
    rows = lax.broadcasted_iota(jnp.int32, c.shape, 0)
    shift = 1
    while shift < seq:
        c = c + jnp.where(rows < shift, 0.0, pltpu.roll(c, shift, 0))
        shift *= 2
    c_ref[...] = c


def _forget_cumsum(p_tail, b_f, bsz, seq):
    n_tail = p_tail.shape[-1]
    b = jnp.pad(b_f, (0, n_tail - b_f.shape[0])).reshape(1, n_tail)
    return pl.pallas_call(
        _cumsum_kernel,
        grid=(bsz,),
        in_specs=[pl.BlockSpec((None, seq, n_tail), lambda i: (i, 0, 0)),
                  pl.BlockSpec((1, n_tail), lambda i: (0, 0))],
        out_specs=pl.BlockSpec((None, seq, n_tail), lambda i: (i, 0, 0)),
        out_shape=jax.ShapeDtypeStruct((bsz, seq, n_tail), F32),
        compiler_params=_params("parallel"),
        name="forget_cumsum",
    )(p_tail.reshape(bsz, seq, n_tail), b)


def _fox_kernel(q_ref, k_ref, v_ref, c_ref, cr_ref, o_ref,
                qs_ref, ks_ref, vs_ref, cc_ref, *, seq, seq_pad):
    dh = q_ref.shape[-1]
    if seq_pad > seq:
        pad = jnp.zeros((seq_pad - seq, dh), BF16)
        qs_ref[pl.ds(seq, seq_pad - seq), :] = pad
        ks_ref[pl.ds(seq, seq_pad - seq), :] = pad
        vs_ref[pl.ds(seq, seq_pad - seq), :] = pad
        cc_ref[pl.ds(seq, seq_pad - seq), :] = jnp.zeros((seq_pad - seq, 1), F32)
    c_all = c_ref[...]
    lane = lax.broadcasted_iota(jnp.int32, c_all.shape, 1)
    cc_ref[pl.ds(0, seq), :] = jnp.sum(jnp.where(lane == pl.program_id(1), c_all, 0.0),
                                       axis=-1, keepdims=True)
    qs_ref[pl.ds(0, seq), :] = q_ref[...]
    ks_ref[pl.ds(0, seq), :] = k_ref[...]
    vs_ref[pl.ds(0, seq), :] = v_ref[...]

    def scores(q0, nq):
        return lax.dot_general(qs_ref[pl.ds(q0, nq), :], ks_ref[pl.ds(0, q0 + nq), :],
                               (((1,), (1,)), ((), ())), preferred_element_type=F32)

    chunks = [(q0, min(FOX_Q_ROWS, seq_pad - q0)) for q0 in range(0, seq_pad, FOX_Q_ROWS)]
    qk_next = scores(*chunks[0])
    for ci, (q0, nq) in enumerate(chunks):
        q1 = q0 + nq
        qk = qk_next
        if ci + 1 < len(chunks):
            qk_next = scores(*chunks[ci + 1])
        u = qk - cr_ref[:, pl.ds(0, q1)] * LOG2E
        rows = lax.broadcasted_iota(jnp.int32, (nq, nq), 0)
        cols = lax.broadcasted_iota(jnp.int32, (nq, nq), 1)
        u_diag = jnp.where(rows >= cols, u[:, q0:], -jnp.inf)
        mu = jnp.max(u_diag, axis=-1, keepdims=True)
        if q0 > 0:
            mu = jnp.maximum(mu, jnp.max(u[:, :q0], axis=-1, keepdims=True))
        ct = cc_ref[pl.ds(q0, nq), :] * LOG2E
        shift = ct - (mu + ct)
        e = jnp.exp2(u_diag + shift)
        if q0 > 0:
            e = jnp.concatenate([jnp.exp2(u[:, :q0] + shift), e], axis=1)
        l = jnp.sum(e, axis=-1, keepdims=True)
        ob = jnp.dot(e.astype(BF16), vs_ref[pl.ds(0, q1), :], preferred_element_type=F32) / l
        n_valid = min(q1, seq) - q0
        if n_valid > 0:
            o_ref[pl.ds(q0, n_valid), :] = ob[:n_valid].astype(BF16)


def _fox_attention(qkv, c, bsz, seq, heads):
    dh = HEAD_FOX
    seq_pad = -(-seq // Q_BLOCK) * Q_BLOCK
    pm = qkv.reshape(bsz, seq, -1)
    n_tail = c.shape[-1]
    ct = jnp.pad(c[:, :, :heads].transpose(0, 2, 1), ((0, 0), (0, 0), (0, seq_pad - seq)))
    c_row = ct[:, :, None, :]

    def seg(s):
        return pl.BlockSpec((None, seq, dh), lambda b, h, s=s: (b, 0, s * heads + h))

    return pl.pallas_call(
        functools.partial(_fox_kernel, seq=seq, seq_pad=seq_pad),
        grid=(bsz, heads),
        in_specs=[
            seg(0), seg(1), seg(2),
            pl.BlockSpec((None, seq, n_tail), lambda b, h: (b, 0, 0)),
            pl.BlockSpec((None, None, 1, seq_pad), lambda b, h: (b, h, 0, 0)),
        ],
        out_specs=pl.BlockSpec((None, seq, dh), lambda b, h: (b, 0, h)),
        out_shape=jax.ShapeDtypeStruct((bsz, seq, heads * dh), BF16),
        scratch_shapes=[pltpu.VMEM((seq_pad, dh), BF16)] * 3 + [pltpu.VMEM((seq_pad, 1), F32)],
        compiler_params=_params("parallel", "parallel"),
        name="fox_attention",
    )(pm, pm, pm, c, c_row)


def _pad_cols(w, n):
    return jnp.pad(w, ((0, 0), (0, n - w.shape[1])))


def _round_up(n, m):
    return -(-n // m) * m


def kernel(x, meta, norm_g, ffn_in, ffn_out, e_w_in, e_conv_w, e_mu, e_w0, e_w2, e_a0, e_a2,
           e_g2, e_k_k, e_k_a, e_r_k, e_lnx_g, e_lnx_b, e_w_out, o_w_in, o_b_f, o_q_g, o_k_g,
           o_w_out):
    bsz, seq_x, d = x.shape
    depth = norm_g.shape[0]
    seq = seq_x + meta.shape[0]
    d_conv = e_conv_w.shape[-1]
    d_rwkv = e_w0.shape[-1]
    h_rwkv = d_rwkv // HEAD_RWKV
    h_fox = d // HEAD_FOX
    n_even_main = 3 * d_conv + 3 * d_rwkv
    n_odd_main = 4 * d

    h = jnp.concatenate([jnp.broadcast_to(meta[None].astype(x.dtype), (bsz,) + meta.shape), x], axis=1)
    h = h.reshape(bsz * seq, d)

    ffn_in_b = ffn_in.astype(BF16)
    ffn_out_b = ffn_out.astype(BF16)
    e_w_in_b = e_w_in.astype(BF16)
    e_w_out_b = e_w_out.astype(BF16)
    o_w_in_b = o_w_in.astype(BF16)
    o_w_out_b = o_w_out.astype(BF16)

    def tail_weights(w_in, n_main):
        n_tail = _round_up(w_in.shape[1] - n_main, LANES)
        return _pad_cols(w_in[:, n_main:], n_tail).astype(BF16)

    for l in range(depth):
        g = norm_g[l]
        i = l // 2
        h = _ffn(h, g[0], ffn_in_b, ffn_out_b, g[1], l, 0)
        if l % 2 == 0:
            assert d_conv == d_rwkv
            mu_rkv = e_mu[i][:3 * d_rwkv].reshape(3, 1, d_rwkv)
            y_a, rkv, p_tail = _proj_even(h, g[2], e_w_in_b, i,
                                          tail_weights(e_w_in[i], n_even_main),
                                          e_conv_w[i], mu_rkv, bsz, seq)
            w, lr, gate = _even_prep(p_tail, e_mu[i][3 * d_rwkv:], e_w0[i], e_w2[i], e_a0[i],
                                     e_a2[i], e_g2[i], bsz, seq)
            y_b = _scan(rkv, w, lr,
                        (e_k_k[i], e_k_a[i], e_r_k[i], e_lnx_g[i], e_lnx_b[i]), h_rwkv)
            h = _out_even(h, y_a, y_b, gate.reshape(bsz * seq, d_rwkv), e_w_out_b, i, g[3],
                          bsz, seq)
        else:
            qkv, og, p_tail = _proj_odd(h, g[2], o_w_in_b, i, tail_weights(o_w_in[i], n_odd_main),
                                        o_q_g[i], o_k_g[i])
            c = _forget_cumsum(p_tail, o_b_f[i], bsz, seq)
            o = _fox_attention(qkv, c, bsz, seq, h_fox)
            h = _out_odd(h, o.reshape(bsz * seq, d), og, o_w_out_b, i, g[3])
        h = _ffn(h, g[4], ffn_in_b, ffn_out_b, g[5], l, 1)
    return h.reshape(bsz, seq, d)[:, meta.shape[0]:]
```
